```python
import math
import jax, jax.numpy as jnp
from jax import lax
import numpy as np

D_MODEL = 1024
BATCH = 16
SEQ = 2048
DEPTH = 2

CHUNK = 64
D_BRANCH = 512
N_BRANCH = 3
S5_GROUP = 16
S5_GROUPS = D_BRANCH // S5_GROUP
S5_STATE = 64
DT_MIN = 1e-3
DT_MAX = 1e-1
POOL_WINDOWS = (2, 4, 8, 16)
POOL_GROUPS = len(POOL_WINDOWS)
POOL_GROUP = D_BRANCH // POOL_GROUPS
SGU_BLOCK = 128
SGU_HEADS = 4
SGU_HEAD_DIM = D_BRANCH // SGU_HEADS
IN_WIDTHS = (D_BRANCH, D_BRANCH, D_BRANCH, D_BRANCH, D_BRANCH, D_BRANCH, D_BRANCH, N_BRANCH * D_MODEL)
D_IN = sum(IN_WIDTHS)
RMS_EPS = 1e-6
LN_EPS = 1e-5

kernel_name = "hybrid_s5_pool_sgu_gated_trunk"


def rmsnorm(x, g):
    xf = x.astype(jnp.float32)
    y = xf * lax.rsqrt(jnp.mean(xf * xf, axis=-1, keepdims=True) + RMS_EPS)
    return (y * g.astype(jnp.float32)).astype(x.dtype)


def s5_mixer(u, lam_re, lam_im, log_dt, b_re, b_im, c_re, c_im, d_skip, w_glu, b_glu):
    bsz, seq, _ = u.shape
    uf = u.astype(jnp.float32).reshape(bsz, seq, S5_GROUPS, S5_GROUP)
    dt = jnp.exp(log_dt.astype(jnp.float32))[:, None]
    lr = lam_re.astype(jnp.float32)
    li = lam_im.astype(jnp.float32)
    mag = jnp.exp(lr * dt)
    ab_re = mag * jnp.cos(li * dt)
    ab_im = mag * jnp.sin(li * dt)
    den = lr * lr + li * li
    nr = ab_re - 1.0
    ni = ab_im
    k_re = (nr * lr + ni * li) / den
    k_im = (ni * lr - nr * li) / den
    br = b_re.astype(jnp.float32)
    bi = b_im.astype(jnp.float32)
    bb_re = k_re[..., None] * br - k_im[..., None] * bi
    bb_im = k_re[..., None] * bi + k_im[..., None] * br
    bu_re = jnp.einsum('gpc,blgc->blgp', bb_re, uf)
    bu_im = jnp.einsum('gpc,blgc->blgp', bb_im, uf)
    a_re = jnp.broadcast_to(ab_re[None, None], (1, seq, S5_GROUPS, S5_STATE))
    a_im = jnp.broadcast_to(ab_im[None, None], (1, seq, S5_GROUPS, S5_STATE))

    def combine(e1, e2):
        a1r, a1i, b1r, b1i = e1
        a2r, a2i, b2r, b2i = e2
        return (a2r * a1r - a2i * a1i,
                a2r * a1i + a2i * a1r,
                a2r * b1r - a2i * b1i + b2r,
                a2r * b1i + a2i * b1r + b2i)

    _, _, h_re, h_im = lax.associative_scan(combine, (a_re, a_im, bu_re, bu_im), axis=1)
    y = (jnp.einsum('gcp,blgp->blgc', c_re.astype(jnp.float32), h_re)
         - jnp.einsum('gcp,blgp->blgc', c_im.astype(jnp.float32), h_im))
    y = y + d_skip.astype(jnp.float32).reshape(S5_GROUPS, S5_GROUP) * uf
    y = jax.nn.gelu(y.reshape(bsz, seq, D_BRANCH))
    y = y * jax.nn.sigmoid(y @ w_glu.astype(jnp.float32) + b_glu.astype(jnp.float32))
    return y.astype(u.dtype)


def pool_mixer(u, w_pool, pool_scale):
    bsz, seq, _ = u.shape
    uf = u.astype(jnp.float32).reshape(bsz, seq, POOL_GROUPS, POOL_GROUP)
    cs = jnp.cumsum(uf, axis=1)
    pos = jnp.arange(1, seq + 1, dtype=jnp.float32)
    outs = []
    for gi, w in enumerate(POOL_WINDOWS):
        c = cs[:, :, gi]
        c_prev = jnp.pad(c, ((0, 0), (w, 0), (0, 0)))[:, :seq]
        mean = (c - c_prev) / jnp.minimum(pos, float(w))[None, :, None]
        outs.append(mean - uf[:, :, gi])
    p = jnp.stack(outs, axis=2)
    y = jnp.einsum('blgc,gcd->blgd', p, w_pool.astype(jnp.float32)).reshape(bsz, seq, D_BRANCH)
    return (y * pool_scale.astype(jnp.float32)).astype(u.dtype)


def sgu_mixer(u, v, ln_g, ln_b, w_s, b_s):
    bsz, seq, _ = v.shape
    vf = v.astype(jnp.float32)
    mu = jnp.mean(vf, axis=-1, keepdims=True)
    var = jnp.mean(jnp.square(vf - mu), axis=-1, keepdims=True)
    vn = (vf - mu) * lax.rsqrt(var + LN_EPS) * ln_g.astype(jnp.float32) + ln_b.astype(jnp.float32)
    vn = vn.reshape(bsz, seq // SGU_BLOCK, SGU_BLOCK, SGU_HEADS, SGU_HEAD_DIM)
    t = jnp.arange(SGU_BLOCK)
    mask = (t[None, :] // CHUNK) <= (t[:, None] // CHUNK)
    ws = jnp.where(mask[None], w_s.astype(jnp.float32), 0.0)
    z = jnp.einsum('hts,bnshc->bnthc', ws, vn) + b_s.astype(jnp.float32).T[None, None, :, :, None]
    z = z.reshape(bsz, seq, D_BRANCH)
    return (u.astype(jnp.float32) * z).astype(u.dtype)


def setup_inputs(seed: int = 0) -> dict:
    key = jax.random.key(seed)
    ks = jax.random.split(key, 24)
    f32 = jnp.float32
    n = jnp.arange(S5_STATE, dtype=f32)
    x = jax.random.normal(ks[0], (BATCH, SEQ, D_MODEL), f32)
    norm_g = 1.0 + 0.02 * jax.random.normal(ks[1], (DEPTH, D_MODEL), f32)
    w_in = jax.random.normal(ks[2], (DEPTH, D_MODEL, D_IN), f32) * D_MODEL ** -0.5
    s5_lam_re = -0.5 + 0.01 * jax.random.normal(ks[3], (DEPTH, S5_GROUPS, S5_STATE), f32)
    s5_lam_im = math.pi * n + 0.01 * jax.random.normal(ks[4], (DEPTH, S5_GROUPS, S5_STATE), f32)
    s5_log_dt = jax.random.uniform(ks[5], (DEPTH, S5_GROUPS), f32, math.log(DT_MIN), math.log(DT_MAX))
    bscale = (2.0 * S5_GROUP) ** -0.5
    s5_b_re = jax.random.normal(ks[6], (DEPTH, S5_GROUPS, S5_STATE, S5_GROUP), f32) * bscale
    s5_b_im = jax.random.normal(ks[7], (DEPTH, S5_GROUPS, S5_STATE, S5_GROUP), f32) * bscale
    cscale = (2.0 * S5_STATE) ** -0.5
    s5_c_re = jax.random.normal(ks[8], (DEPTH, S5_GROUPS, S5_GROUP, S5_STATE), f32) * cscale
    s5_c_im = jax.random.normal(ks[9], (DEPTH, S5_GROUPS, S5_GROUP, S5_STATE), f32) * cscale
    s5_d = jax.random.normal(ks[10], (DEPTH, D_BRANCH), f32)
    s5_w_glu = jax.random.normal(ks[11], (DEPTH, D_BRANCH, D_BRANCH), f32) * D_BRANCH ** -0.5
    s5_b_glu = 0.02 * jax.random.normal(ks[12], (DEPTH, D_BRANCH), f32)
    pool_w = jax.random.normal(ks[13], (DEPTH, POOL_GROUPS, POOL_GROUP, POOL_GROUP), f32) * POOL_GROUP ** -0.5
    pool_scale = 1.0 + 0.02 * jax.random.normal(ks[14], (DEPTH, D_BRANCH), f32)
    sgu_ln_g = 1.0 + 0.02 * jax.random.normal(ks[15], (DEPTH, D_BRANCH), f32)
    sgu_ln_b = 0.02 * jax.random.normal(ks[16], (DEPTH, D_BRANCH), f32)
    sgu_w = jax.random.normal(ks[17], (DEPTH, SGU_HEADS, SGU_BLOCK, SGU_BLOCK), f32) * SGU_BLOCK ** -0.5
    sgu_b = 1.0 + 0.02 * jax.random.normal(ks[18], (DEPTH, SGU_HEADS, SGU_BLOCK), f32)
    w_branch = jax.random.normal(ks[19], (DEPTH, N_BRANCH, D_BRANCH, D_MODEL), f32) * D_BRANCH ** -0.5
    w_out = jax.random.normal(ks[20], (DEPTH, D_MODEL, D_MODEL), f32) * D_MODEL ** -0.5
    final_norm_g = 1.0 + 0.02 * jax.random.normal(ks[21], (D_MODEL,), f32)
    return {"x": x, "norm_g": norm_g, "w_in": w_in,
            "s5_lam_re": s5_lam_re, "s5_lam_im": s5_lam_im, "s5_log_dt": s5_log_dt,
            "s5_b_re": s5_b_re, "s5_b_im": s5_b_im, "s5_c_re": s5_c_re, "s5_c_im": s5_c_im,
            "s5_d": s5_d, "s5_w_glu": s5_w_glu, "s5_b_glu": s5_b_glu,
            "pool_w": pool_w, "pool_scale": pool_scale,
            "sgu_ln_g": sgu_ln_g, "sgu_ln_b": sgu_ln_b, "sgu_w": sgu_w, "sgu_b": sgu_b,
            "w_branch": w_branch, "w_out": w_out, "final_norm_g": final_norm_g}


def reference(x, norm_g, w_in, s5_lam_re, s5_lam_im, s5_log_dt, s5_b_re, s5_b_im, s5_c_re, s5_c_im,
              s5_d, s5_w_glu, s5_b_glu, pool_w, pool_scale, sgu_ln_g, sgu_ln_b, sgu_w, sgu_b,
              w_branch, w_out, final_norm_g):
    bsz, seq, _ = x.shape
    split_idx = [int(v) for v in np.cumsum(IN_WIDTHS)[:-1]]
    for l in range(DEPTH):
        h = rmsnorm(x, norm_g[l])
        z = h @ w_in[l]
        a_val, a_gate, b_val, b_gate, c_u, c_v, c_gate, gates = jnp.split(z, split_idx, axis=-1)
        ya = s5_mixer(a_val, s5_lam_re[l], s5_lam_im[l], s5_log_dt[l], s5_b_re[l], s5_b_im[l],
                      s5_c_re[l], s5_c_im[l], s5_d[l], s5_w_glu[l], s5_b_glu[l]) * jax.nn.silu(a_gate)
        yb = pool_mixer(b_val, pool_w[l], pool_scale[l]) * jax.nn.silu(b_gate)
        yc = sgu_mixer(c_u, c_v, sgu_ln_g[l], sgu_ln_b[l], sgu_w[l], sgu_b[l]) * jax.nn.silu(c_gate)
        ys = jnp.stack([ya, yb, yc], axis=2)
        proj = jnp.einsum('blkc,kcd->blkd', ys, w_branch[l])
        g = jax.nn.sigmoid(gates.reshape(bsz, seq, N_BRANCH, D_MODEL))
        merged = jnp.sum(g * proj, axis=2)
        x = x + merged @ w_out[l]
    return rmsnorm(x, final_norm_g)
```

```python
import functools

import jax
import jax.numpy as jnp
from jax import lax
from jax.experimental import pallas as pl
from jax.experimental.pallas import tpu as pltpu

D_MODEL = 1024
BATCH = 16
SEQ = 2048
DEPTH = 2
CHUNK = 64
D_BRANCH = 512
N_BRANCH = 3
S5_GROUP = 16
S5_GROUPS = D_BRANCH // S5_GROUP
S5_STATE = 64
POOL_WINDOWS = (2, 4, 8, 16)
POOL_GROUP = D_BRANCH // len(POOL_WINDOWS)
SGU_BLOCK = 128
SGU_HEADS = 4
SGU_HEAD_DIM = D_BRANCH // SGU_HEADS
RMS_EPS = 1e-6
LN_EPS = 1e-5

LANES = 128
SUBLANES = 8
S5_BLOCKS = D_BRANCH // LANES
S5_BLOCK_STATE = (LANES // S5_GROUP) * S5_STATE
T_TILE = CHUNK
ROWS = T_TILE * BATCH
ROW_CHUNK = 256
N_CHUNKS = ROWS // ROW_CHUNK
HALO_ROWS = max(POOL_WINDOWS) * BATCH
VMEM_LIMIT_BYTES = 62 * 1024 * 1024

OFF_A_VAL, OFF_A_GATE, OFF_B_VAL, OFF_B_GATE, OFF_C_U, OFF_C_V, OFF_C_GATE, OFF_GATES = (
    0, 512, 1024, 1536, 2048, 2560, 3072, 3584)

F32 = jnp.float32
BF16 = jnp.bfloat16


def _rows(c, n=ROW_CHUNK, base=0):
    return pl.ds(pl.multiple_of(base + c * n, SUBLANES), n)


def _dot(a, b):
    return jnp.dot(a, b, preferred_element_type=F32)


def _for_chunks(body):
    def wrapped(c, carry):
        body(c)
        return carry
    lax.fori_loop(0, N_CHUNKS, wrapped, 0)


def _layer_kernel(x_ref, g_ref, win_ref, wb_ref, are_ref, aim_ref, wc_ref, dsk_ref, wglu_ref, bglu_ref,
                  pw_ref, psc_ref, lng_ref, lnb_ref, sw_ref, sb_ref, wbr_ref, wout_ref, fg_ref,
                  o_ref,
                  h_scr, ext_scr, bu_scr, hs_scr, st_scr, vn_scr, vsave_scr, z_scr, ycat_scr,
                  *, final):
    i = pl.program_id(0)

    @pl.when(i == 0)
    def _():
        st_scr[...] = jnp.zeros_like(st_scr)
        ext_scr[0:HALO_ROWS, :] = jnp.zeros((HALO_ROWS, D_BRANCH), F32)
        vsave_scr[...] = jnp.zeros_like(vsave_scr)

    def norm_body(c):
        r = _rows(c)
        xc = x_ref[r, :]
        ms = jnp.mean(xc * xc, axis=-1, keepdims=True)
        h_scr[r, :] = (xc * lax.rsqrt(ms + RMS_EPS) * g_ref[...]).astype(BF16)
    _for_chunks(norm_body)

    def aval_body(c):
        ext_scr[_rows(c, base=HALO_ROWS), :] = _dot(h_scr[_rows(c), :], win_ref[:, OFF_A_VAL:OFF_A_VAL + D_BRANCH])
    _for_chunks(aval_body)

    half = S5_BLOCK_STATE
    n_sub = BATCH // SUBLANES
    for j in range(S5_BLOCKS):
        def bu_body(c, j=j):
            u = ext_scr[_rows(c, base=HALO_ROWS), j * LANES:(j + 1) * LANES].astype(BF16)
            bu_scr[_rows(c), :] = _dot(u, wb_ref[j])
        _for_chunks(bu_body)

        a_re = jnp.broadcast_to(are_ref[j:j + 1, :], (SUBLANES, half))
        a_im = jnp.broadcast_to(aim_ref[j:j + 1, :], (SUBLANES, half))

        def scan_body(t, carry, a_re=a_re, a_im=a_im):
            r0 = pl.multiple_of(t * BATCH, BATCH)
            new = []
            for s in range(n_sub):
                h_re, h_im = carry[2 * s], carry[2 * s + 1]
                bu = bu_scr[pl.ds(r0 + SUBLANES * s, SUBLANES), :]
                new.append(a_re * h_re - a_im * h_im + bu[:, :half])
                new.append(a_re * h_im + a_im * h_re + bu[:, half:])
            full = jnp.concatenate(
                [jnp.concatenate([new[2 * s], new[2 * s + 1]], axis=1) for s in range(n_sub)], axis=0)
            hs_scr[pl.ds(r0, BATCH), :] = full.astype(BF16)
            return tuple(new)

        init = []
        for s in range(n_sub):
            rs = slice(SUBLANES * s, SUBLANES * (s + 1))
            init += [st_scr[j, rs, 0:half], st_scr[j, rs, half:2 * half]]
        fin = lax.fori_loop(0, T_TILE, scan_body, tuple(init), unroll=4)
        for s in range(n_sub):
            rs = slice(SUBLANES * s, SUBLANES * (s + 1))
            st_scr[j, rs, 0:half] = fin[2 * s]
            st_scr[j, rs, half:2 * half] = fin[2 * s + 1]

        def y_body(c, j=j):
            z_scr[j, _rows(c), :] = _dot(hs_scr[_rows(c), :], wc_ref[j])
        _for_chunks(y_body)

    def a_out_body(c):
        r = _rows(c)
        y = jnp.concatenate([z_scr[j, r, :] for j in range(S5_BLOCKS)], axis=1)
        y = y + dsk_ref[...] * ext_scr[_rows(c, base=HALO_ROWS), :]
        y = jax.nn.gelu(y)
        y = y * jax.nn.sigmoid(_dot(y.astype(BF16), wglu_ref[...]) + bglu_ref[...])
        gate = _dot(h_scr[r, :], win_ref[:, OFF_A_GATE:OFF_A_GATE + D_BRANCH])
        ycat_scr[0, r, :] = (y * jax.nn.silu(gate)).astype(BF16)
    _for_chunks(a_out_body)

    def bval_body(c):
        ext_scr[_rows(c, base=HALO_ROWS), :] = _dot(h_scr[_rows(c), :], win_ref[:, OFF_B_VAL:OFF_B_VAL + D_BRANCH])
    _for_chunks(bval_body)

    def b_out_body(c):
        r = _rows(c)
        t_idx = lax.broadcasted_iota(jnp.int32, (ROW_CHUNK, 1), 0) // BATCH
        pos = (i * T_TILE + c * (ROW_CHUNK // BATCH) + 1 + t_idx).astype(F32)
        ys = []
        for gi, w in enumerate(POOL_WINDOWS):
            cols = slice(gi * POOL_GROUP, (gi + 1) * POOL_GROUP)
            cur = ext_scr[_rows(c, base=HALO_ROWS), cols]
            acc = cur
            for k in range(1, w):
                acc = acc + ext_scr[_rows(c, base=HALO_ROWS - k * BATCH), cols]
            p = acc / jnp.minimum(pos, float(w)) - cur
            ys.append(_dot(p.astype(BF16), pw_ref[gi]))
        y = jnp.concatenate(ys, axis=1) * psc_ref[...]
        gate = _dot(h_scr[r, :], win_ref[:, OFF_B_GATE:OFF_B_GATE + D_BRANCH])
        ycat_scr[1, r, :] = (y * jax.nn.silu(gate)).astype(BF16)
    _for_chunks(b_out_body)

    ext_scr[0:HALO_ROWS, :] = ext_scr[ROWS:ROWS + HALO_ROWS, :]

    def cv_body(c):
        r = _rows(c)
        v = _dot(h_scr[r, :], win_ref[:, OFF_C_V:OFF_C_V + D_BRANCH])
        mu = jnp.mean(v, axis=-1, keepdims=True)
        vc = v - mu
        var = jnp.mean(vc * vc, axis=-1, keepdims=True)
        vn = vc * lax.rsqrt(var + LN_EPS) * lng_ref[...] + lnb_ref[...]
        for hd in range(SGU_HEADS):
            vn_scr[hd, r, :] = vn[:, hd * SGU_HEAD_DIM:(hd + 1) * SGU_HEAD_DIM]
    _for_chunks(cv_body)

    second = (i % 2) == 1
    t0 = pl.multiple_of((i % 2) * CHUNK, CHUNK)
    for hd in range(SGU_HEADS):
        w_rows = sw_ref[hd, pl.ds(t0, CHUNK), :]
        w_lo, w_hi = w_rows[:, 0:CHUNK], w_rows[:, CHUNK:2 * CHUNK]
        w_cur = jnp.where(second, w_hi, w_lo).astype(BF16)
        w_prev = jnp.where(second, w_lo, 0.0).astype(BF16)
        v_cur = jnp.concatenate(
            [vn_scr[hd, pl.ds(b, CHUNK, stride=BATCH), :] for b in range(BATCH)], axis=1).astype(BF16)
        zt = _dot(w_cur, v_cur) + _dot(w_prev, vsave_scr[hd]) + sb_ref[hd, pl.ds(t0, CHUNK), :]
        vsave_scr[hd] = v_cur
        for b in range(BATCH):
            z_scr[hd, pl.ds(b, CHUNK, stride=BATCH), :] = zt[:, b * SGU_HEAD_DIM:(b + 1) * SGU_HEAD_DIM]

    def c_out_body(c):
        r = _rows(c)
        hc = h_scr[r, :]
        z = jnp.concatenate([z_scr[hd, r, :] for hd in range(SGU_HEADS)], axis=1)
        u = _dot(hc, win_ref[:, OFF_C_U:OFF_C_U + D_BRANCH])
        gate = _dot(hc, win_ref[:, OFF_C_GATE:OFF_C_GATE + D_BRANCH])
        ycat_scr[2, r, :] = (u * z * jax.nn.silu(gate)).astype(BF16)
    _for_chunks(c_out_body)

    def out_body(c):
        r = _rows(c)
        hc = h_scr[r, :]
        merged = None
        for k in range(N_BRANCH):
            mg = _dot(hc, win_ref[:, OFF_GATES + k * D_MODEL:OFF_GATES + (k + 1) * D_MODEL])
            term = jax.nn.sigmoid(mg) * _dot(ycat_scr[k, r, :], wbr_ref[k])
            merged = term if merged is None else merged + term
        out = x_ref[r, :] + _dot(merged.astype(BF16), wout_ref[...])
        if final:
            ms = jnp.mean(out * out, axis=-1, keepdims=True)
            out = out * lax.rsqrt(ms + RMS_EPS) * fg_ref[...]
        o_ref[r, :] = out
    _for_chunks(out_body)


def _const_spec(shape):
    nd = len(shape)
    return pl.BlockSpec(shape, lambda i, _nd=nd: (0,) * _nd, pipeline_mode=pl.Buffered(1))


def _layer_call(xt, params, final):
    n_rows = xt.shape[0]
    in_specs = [pl.BlockSpec((ROWS, D_MODEL), lambda i: (i, 0))] + [_const_spec(p.shape) for p in params]
    scratch = [
        pltpu.VMEM((ROWS, D_MODEL), BF16),
        pltpu.VMEM((HALO_ROWS + ROWS, D_BRANCH), F32),
        pltpu.VMEM((ROWS, 2 * S5_BLOCK_STATE), F32),
        pltpu.VMEM((ROWS, 2 * S5_BLOCK_STATE), BF16),
        pltpu.VMEM((S5_BLOCKS, BATCH, 2 * S5_BLOCK_STATE), F32),
        pltpu.VMEM((SGU_HEADS, ROWS, SGU_HEAD_DIM), F32),
        pltpu.VMEM((SGU_HEADS, CHUNK, BATCH * SGU_HEAD_DIM), BF16),
        pltpu.VMEM((SGU_HEADS, ROWS, SGU_HEAD_DIM), F32),
        pltpu.VMEM((N_BRANCH, ROWS, D_BRANCH), BF16),
    ]
    return pl.pallas_call(
        functools.partial(_layer_kernel, final=final),
        grid=(n_rows // ROWS,),
        in_specs=in_specs,
        out_specs=pl.BlockSpec((ROWS, D_MODEL), lambda i: (i, 0)),
        out_shape=jax.ShapeDtypeStruct(xt.shape, F32),
        scratch_shapes=scratch,
        compiler_params=pltpu.CompilerParams(
            dimension_semantics=("arbitrary",), vmem_limit_bytes=VMEM_LIMIT_BYTES),
        name="trunk_layer_final" if final else "trunk_layer",
    )(xt, *params)


def _block_diag(blocks):
    n = blocks.shape[1]
    eye = jnp.eye(n, dtype=blocks.dtype)
    out = jnp.einsum('jgrc,gh->jgrhc', blocks, eye)
    return out.reshape(blocks.shape[0], n * blocks.shape[2], n * blocks.shape[3])


def _s5_params(lam_re, lam_im, log_dt, b_re, b_im, c_re, c_im):
    dt = jnp.exp(log_dt)[:, None]
    mag = jnp.exp(lam_re * dt)
    ab_re = mag * jnp.cos(lam_im * dt)
    ab_im = mag * jnp.sin(lam_im * dt)
    den = lam_re * lam_re + lam_im * lam_im
    nr = ab_re - 1.0
    ni = ab_im
    k_re = (nr * lam_re + ni * lam_im) / den
    k_im = (ni * lam_re - nr * lam_im) / den
    bb_re = k_re[..., None] * b_re - k_im[..., None] * b_im
    bb_im = k_re[..., None] * b_im + k_im[..., None] * b_re
    gpb = LANES // S5_GROUP

    def blk(a):
        return a.reshape((S5_BLOCKS, gpb) + a.shape[1:])
    wb = jnp.concatenate([_block_diag(blk(bb_re.transpose(0, 2, 1))),
                          _block_diag(blk(bb_im.transpose(0, 2, 1)))], axis=2)
    wc = jnp.concatenate([_block_diag(blk(c_re.transpose(0, 2, 1))),
                          _block_diag(blk(-c_im.transpose(0, 2, 1)))], axis=1)
    return (wb.astype(BF16), ab_re.reshape(S5_BLOCKS, S5_BLOCK_STATE), ab_im.reshape(S5_BLOCKS, S5_BLOCK_STATE),
            wc.astype(BF16))


def kernel(x, norm_g, w_in, s5_lam_re, s5_lam_im, s5_log_dt, s5_b_re, s5_b_im, s5_c_re, s5_c_im, s5_d, s5_w_glu, s5_b_glu, pool_w, pool_scale, sgu_ln_g, sgu_ln_b, sgu_w, sgu_b, w_branch, w_out, final_norm_g):
    bsz, seq, d = x.shape
    assert (bsz, seq, d) == (BATCH, SEQ, D_MODEL)
    xt = x.transpose(1, 0, 2).reshape(seq * bsz, d)
    for l in range(DEPTH):
        wb, a_re, a_im, wc = _s5_params(s5_lam_re[l], s5_lam_im[l], s5_log_dt[l], s5_b_re[l], s5_b_im[l],
                                        s5_c_re[l], s5_c_im[l])
        params = (
            norm_g[l].reshape(1, d), w_in[l].astype(BF16), wb, a_re, a_im, wc,
            s5_d[l].reshape(1, D_BRANCH), s5_w_glu[l].astype(BF16), s5_b_glu[l].reshape(1, D_BRANCH),
            pool_w[l].astype(BF16), pool_scale[l].reshape(1, D_BRANCH),
            sgu_ln_g[l].reshape(1, D_BRANCH), sgu_ln_b[l].reshape(1, D_BRANCH),
            sgu_w[l], sgu_b[l].reshape(SGU_HEADS, SGU_BLOCK, 1),
            w_branch[l].astype(BF16), w_out[l].astype(BF16), final_norm_g.reshape(1, d),
        )
        xt = _layer_call(xt, params, final=(l == DEPTH - 1))
    return xt.reshape(seq, bsz, d).transpose(1, 0, 2)
```

```python
import functools

import jax
import jax.numpy as jnp
from jax import lax
from jax.experimental import pallas as pl
from jax.experimental.pallas import tpu as pltpu

D_MODEL = 1024
BATCH = 16
SEQ = 2048
DEPTH = 2
CHUNK = 64
D_BRANCH = 512
N_BRANCH = 3
S5_GROUP = 16
S5_GROUPS = D_BRANCH // S5_GROUP
S5_STATE = 64
POOL_WINDOWS = (2, 4, 8, 16)
POOL_GROUP = D_BRANCH // len(POOL_WINDOWS)
SGU_BLOCK = 128
SGU_HEADS = 4
SGU_HEAD_DIM = D_BRANCH // SGU_HEADS
RMS_EPS = 1e-6
LN_EPS = 1e-5

LANES = 128
SUBLANES = 8
S5_BLOCKS = D_BRANCH // LANES
S5_BLOCK_STATE = (LANES // S5_GROUP) * S5_STATE
T_TILE = CHUNK
ROWS = T_TILE * BATCH
ROW_CHUNK = 256
N_CHUNKS = ROWS // ROW_CHUNK
T_CHUNK = ROW_CHUNK // BATCH
HALO_ROWS = max(POOL_WINDOWS) * BATCH
N_SUB = BATCH // SUBLANES
VMEM_LIMIT_BYTES = 62 * 1024 * 1024

OFF_A_VAL, OFF_A_GATE, OFF_B_VAL, OFF_B_GATE, OFF_C_U, OFF_C_V, OFF_C_GATE, OFF_GATES = (
    0, 512, 1024, 1536, 2048, 2560, 3072, 3584)

F32 = jnp.float32
BF16 = jnp.bfloat16


def _rows(c, n=ROW_CHUNK, base=0):
    return pl.ds(pl.multiple_of(base + c * n, SUBLANES), n)


def _dot(a, b):
    return jnp.dot(a, b, preferred_element_type=F32)


def _for_chunks(body):
    def wrapped(c, carry):
        body(c)
        return carry
    lax.fori_loop(0, N_CHUNKS, wrapped, 0)


def _layer_kernel(x_ref, g_ref, win_ref, wb_ref, are_ref, aim_ref, wc_ref, dsk_ref, wglu_ref, bglu_ref,
                  pw_ref, psc_ref, lng_ref, lnb_ref, sw_ref, sb_ref, wbr_ref, wout_ref, fg_ref,
                  o_ref,
                  h_scr, ext_scr, st_scr, vn_scr, vsave_scr, z_scr, ya_scr,
                  *, final):
    i = pl.program_id(0)
    half = S5_BLOCK_STATE

    @pl.when(i == 0)
    def _():
        st_scr[...] = jnp.zeros_like(st_scr)
        ext_scr[0:HALO_ROWS, :] = jnp.zeros((HALO_ROWS, D_BRANCH), F32)
        vsave_scr[...] = jnp.zeros_like(vsave_scr)

    def loop_a(c):
        r = _rows(c)
        xc = x_ref[r, :]
        ms = jnp.mean(xc * xc, axis=-1, keepdims=True)
        hc = (xc * lax.rsqrt(ms + RMS_EPS) * g_ref[...]).astype(BF16)
        h_scr[r, :] = hc

        ext_scr[_rows(c, base=HALO_ROWS), :] = _dot(hc, win_ref[:, OFF_B_VAL:OFF_B_VAL + D_BRANCH])

        v = _dot(hc, win_ref[:, OFF_C_V:OFF_C_V + D_BRANCH])
        mu = jnp.mean(v, axis=-1, keepdims=True)
        vc = v - mu
        var = jnp.mean(vc * vc, axis=-1, keepdims=True)
        vn = vc * lax.rsqrt(var + LN_EPS) * lng_ref[...] + lnb_ref[...]
        for hd in range(SGU_HEADS):
            vn_scr[hd, r, :] = vn[:, hd * SGU_HEAD_DIM:(hd + 1) * SGU_HEAD_DIM]

        aval = _dot(hc, win_ref[:, OFF_A_VAL:OFF_A_VAL + D_BRANCH])
        ys = []
        for j in range(S5_BLOCKS):
            bu = _dot(aval[:, j * LANES:(j + 1) * LANES].astype(BF16), wb_ref[j])
            a_re, a_im = are_ref[j], aim_ref[j]
            state = []
            for s in range(N_SUB):
                rs = slice(SUBLANES * s, SUBLANES * (s + 1))
                state += [st_scr[j, rs, 0:half], st_scr[j, rs, half:2 * half]]
            hs_rows = []
            for t in range(T_CHUNK):
                for s in range(N_SUB):
                    h_re, h_im = state[2 * s], state[2 * s + 1]
                    b = bu[t * BATCH + s * SUBLANES:t * BATCH + (s + 1) * SUBLANES, :]
                    state[2 * s] = a_re * h_re - a_im * h_im + b[:, :half]
                    state[2 * s + 1] = a_re * h_im + a_im * h_re + b[:, half:]
                    hs_rows.append(jnp.concatenate([state[2 * s], state[2 * s + 1]], axis=1))
            for s in range(N_SUB):
                rs = slice(SUBLANES * s, SUBLANES * (s + 1))
                st_scr[j, rs, 0:half] = state[2 * s]
                st_scr[j, rs, half:2 * half] = state[2 * s + 1]
            hs = jnp.concatenate(hs_rows, axis=0).astype(BF16)
            ys.append(_dot(hs, wc_ref[j]))
        y = jnp.concatenate(ys, axis=1) + dsk_ref[...] * aval
        y = jax.nn.gelu(y)
        y = y * jax.nn.sigmoid(_dot(y.astype(BF16), wglu_ref[...]) + bglu_ref[...])
        gate = _dot(hc, win_ref[:, OFF_A_GATE:OFF_A_GATE + D_BRANCH])
        ya_scr[r, :] = (y * jax.nn.silu(gate)).astype(BF16)
    _for_chunks(loop_a)

    second = (i % 2) == 1
    t0 = pl.multiple_of((i % 2) * CHUNK, CHUNK)
    for hd in range(SGU_HEADS):
        w_rows = sw_ref[hd, pl.ds(t0, CHUNK), :]
        w_lo, w_hi = w_rows[:, 0:CHUNK], w_rows[:, CHUNK:2 * CHUNK]
        w_cur = jnp.where(second, w_hi, w_lo).astype(BF16)
        w_prev = jnp.where(second, w_lo, 0.0).astype(BF16)
        v_cur = jnp.concatenate(
            [vn_scr[hd, pl.ds(b, CHUNK, stride=BATCH), :] for b in range(BATCH)], axis=1).astype(BF16)
        zt = _dot(w_cur, v_cur) + _dot(w_prev, vsave_scr[hd]) + sb_ref[hd, pl.ds(t0, CHUNK), :]
        vsave_scr[hd] = v_cur
        for b in range(BATCH):
            z_scr[hd, pl.ds(b, CHUNK, stride=BATCH), :] = zt[:, b * SGU_HEAD_DIM:(b + 1) * SGU_HEAD_DIM]

    def loop_b(c):
        r = _rows(c)
        hc = h_scr[r, :]

        t_idx = lax.broadcasted_iota(jnp.int32, (ROW_CHUNK, 1), 0) // BATCH
        pos = (i * T_TILE + c * T_CHUNK + 1 + t_idx).astype(F32)
        ys = []
        for gi, w in enumerate(POOL_WINDOWS):
            cols = slice(gi * POOL_GROUP, (gi + 1) * POOL_GROUP)
            cur = ext_scr[_rows(c, base=HALO_ROWS), cols]
            acc = cur
            for k in range(1, w):
                acc = acc + ext_scr[_rows(c, base=HALO_ROWS - k * BATCH), cols]
            p = acc / jnp.minimum(pos, float(w)) - cur
            ys.append(_dot(p.astype(BF16), pw_ref[gi]))
        y = jnp.concatenate(ys, axis=1) * psc_ref[...]
        gate = _dot(hc, win_ref[:, OFF_B_GATE:OFF_B_GATE + D_BRANCH])
        yb = (y * jax.nn.silu(gate)).astype(BF16)

        z = jnp.concatenate([z_scr[hd, r, :] for hd in range(SGU_HEADS)], axis=1)
        u = _dot(hc, win_ref[:, OFF_C_U:OFF_C_U + D_BRANCH])
        gate = _dot(hc, win_ref[:, OFF_C_GATE:OFF_C_GATE + D_BRANCH])
        yc = (u * z * jax.nn.silu(gate)).astype(BF16)

        merged = None
        for k, yk in enumerate((ya_scr[r, :], yb, yc)):
            mg = _dot(hc, win_ref[:, OFF_GATES + k * D_MODEL:OFF_GATES + (k + 1) * D_MODEL])
            term = jax.nn.sigmoid(mg) * _dot(yk, wbr_ref[k])
            merged = term if merged is None else merged + term
        out = x_ref[r, :] + _dot(merged.astype(BF16), wout_ref[...])
        if final:
            ms = jnp.mean(out * out, axis=-1, keepdims=True)
            out = out * lax.rsqrt(ms + RMS_EPS) * fg_ref[...]
        o_ref[r, :] = out
    _for_chunks(loop_b)

    ext_scr[0:HALO_ROWS, :] = ext_scr[ROWS:ROWS + HALO_ROWS, :]


def _const_spec(shape):
    nd = len(shape)
    return pl.BlockSpec(shape, lambda i, _nd=nd: (0,) * _nd, pipeline_mode=pl.Buffered(1))


def _layer_call(xt, params, final):
    n_rows = xt.shape[0]
    in_specs = [pl.BlockSpec((ROWS, D_MODEL), lambda i: (i, 0))] + [_const_spec(p.shape) for p in params]
    scratch = [
        pltpu.VMEM((ROWS, D_MODEL), BF16),
        pltpu.VMEM((HALO_ROWS + ROWS, D_BRANCH), F32),
        pltpu.VMEM((S5_BLOCKS, BATCH, 2 * S5_BLOCK_STATE), F32),
        pltpu.VMEM((SGU_HEADS, ROWS, SGU_HEAD_DIM), F32),
        pltpu.VMEM((SGU_HEADS, CHUNK, BATCH * SGU_HEAD_DIM), BF16),
        pltpu.VMEM((SGU_HEADS, ROWS, SGU_HEAD_DIM), F32),
        pltpu.VMEM((ROWS, D_BRANCH), BF16),
    ]
    return pl.pallas_call(
        functools.partial(_layer_kernel, final=final),
        grid=(n_rows // ROWS,),
        in_specs=in_specs,
        out_specs=pl.BlockSpec((ROWS, D_MODEL), lambda i: (i, 0)),
        out_shape=jax.ShapeDtypeStruct(xt.shape, F32),
        scratch_shapes=scratch,
        compiler_params=pltpu.CompilerParams(
            dimension_semantics=("arbitrary",), vmem_limit_bytes=VMEM_LIMIT_BYTES),
        name="trunk_layer_final" if final else "trunk_layer",
    )(xt, *params)


def _block_diag(blocks):
    n = blocks.shape[1]
    eye = jnp.eye(n, dtype=blocks.dtype)
    out = jnp.einsum('jgrc,gh->jgrhc', blocks, eye)
    return out.reshape(blocks.shape[0], n * blocks.shape[2], n * blocks.shape[3])


def _s5_params(lam_re, lam_im, log_dt, b_re, b_im, c_re, c_im):
    dt = jnp.exp(log_dt)[:, None]
    mag = jnp.exp(lam_re * dt)
    ab_re = mag * jnp.cos(lam_im * dt)
    ab_im = mag * jnp.sin(lam_im * dt)
    den = lam_re * lam_re + lam_im * lam_im
    nr = ab_re - 1.0
    ni = ab_im
    k_re = (nr * lam_re + ni * lam_im) / den
    k_im = (ni * lam_re - nr * lam_im) / den
    bb_re = k_re[..., None] * b_re - k_im[..., None] * b_im
    bb_im = k_re[..., None] * b_im + k_im[..., None] * b_re
    gpb = LANES // S5_GROUP

    def blk(a):
        return a.reshape((S5_BLOCKS, gpb) + a.shape[1:])

    def rep(a):
        return jnp.broadcast_to(a.reshape(S5_BLOCKS, 1, S5_BLOCK_STATE), (S5_BLOCKS, SUBLANES, S5_BLOCK_STATE))
    wb = jnp.concatenate([_block_diag(blk(bb_re.transpose(0, 2, 1))),
                          _block_diag(blk(bb_im.transpose(0, 2, 1)))], axis=2)
    wc = jnp.concatenate([_block_diag(blk(c_re.transpose(0, 2, 1))),
                          _block_diag(blk(-c_im.transpose(0, 2, 1)))], axis=1)
    return wb.astype(BF16), rep(ab_re), rep(ab_im), wc.astype(BF16)


def kernel(x, norm_g, w_in, s5_lam_re, s5_lam_im, s5_log_dt, s5_b_re, s5_b_im, s5_c_re, s5_c_im, s5_d, s5_w_glu, s5_b_glu, pool_w, pool_scale, sgu_ln_g, sgu_ln_b, sgu_w, sgu_b, w_branch, w_out, final_norm_g):
    bsz, seq, d = x.shape
    assert (bsz, seq, d) == (BATCH, SEQ, D_MODEL)
    xt = x.transpose(1, 0, 2).reshape(seq * bsz, d)
    for l in range(DEPTH):
        wb, a_re, a_im, wc = _s5_params(s5_lam_re[l], s5_lam_im[l], s5_log_dt[l], s5_b_re[l], s5_b_im[l],
                                        s5_c_re[l], s5_c_im[l])
        params = (
            norm_g[l].reshape(1, d), w_in[l].astype(BF16), wb, a_re, a_im, wc,
            s5_d[l].reshape(1, D_BRANCH), s5_w_glu[l].astype(BF16), s5_b_glu[l].reshape(1, D_BRANCH),
            pool_w[l].astype(BF16), pool_scale[l].reshape(1, D_BRANCH),
            sgu_ln_g[l].reshape(1, D_BRANCH), sgu_ln_b[l].reshape(1, D_BRANCH),
            sgu_w[l], sgu_b[l].reshape(SGU_HEADS, SGU_BLOCK, 1),
            w_branch[l].astype(BF16), w_out[l].astype(BF16), final_norm_g.reshape(1, d),
        )
        xt = _layer_call(xt, params, final=(l == DEPTH - 1))
    return xt.reshape(seq, bsz, d).transpose(1, 0, 2)
```

```python
import functools

import jax
import jax.numpy as jnp
from jax import lax
from jax.experimental import pallas as pl
from jax.experimental.pallas import tpu as pltpu

D_MODEL = 1024
BATCH = 16
SEQ = 2048
DEPTH = 2
CHUNK = 64
D_BRANCH = 512
N_BRANCH = 3
S5_GROUP = 16
S5_GROUPS = D_BRANCH // S5_GROUP
S5_STATE = 64
POOL_WINDOWS = (2, 4, 8, 16)
POOL_GROUP = D_BRANCH // len(POOL_WINDOWS)
SGU_BLOCK = 128
SGU_HEADS = 4
SGU_HEAD_DIM = D_BRANCH // SGU_HEADS
RMS_EPS = 1e-6
LN_EPS = 1e-5

LANES = 128
SUBLANES = 8
S5_BLOCKS = D_BRANCH // LANES
S5_BLOCK_STATE = (LANES // S5_GROUP) * S5_STATE
T_TILE = CHUNK
ROWS = T_TILE * BATCH
ROW_CHUNK = 512
N_CHUNKS = ROWS // ROW_CHUNK
T_CHUNK = ROW_CHUNK // BATCH
HALO_ROWS = max(POOL_WINDOWS) * BATCH
N_SUB = BATCH // SUBLANES
VMEM_LIMIT_BYTES = 62 * 1024 * 1024

OFF_A_VAL, OFF_A_GATE, OFF_B_VAL, OFF_B_GATE, OFF_C_U, OFF_C_V, OFF_C_GATE, OFF_GATES = (
    0, 512, 1024, 1536, 2048, 2560, 3072, 3584)

F32 = jnp.float32
BF16 = jnp.bfloat16


def _rows(c, n=ROW_CHUNK, base=0):
    if isinstance(c, int):
        return pl.ds(base + c * n, n)
    return pl.ds(pl.multiple_of(base + c * n, SUBLANES), n)


def _dot(a, b):
    return jnp.dot(a, b, preferred_element_type=F32)


def _for_chunks(body):
    def wrapped(c, carry):
        body(c)
        return carry
    lax.fori_loop(0, N_CHUNKS, wrapped, 0)


def _layer_kernel(x_ref, g_ref, win_ref, wb_ref, are_ref, aim_ref, wc_ref, dsk_ref, wglu_ref, bglu_ref,
                  pw_ref, psc_ref, lng_ref, lnb_ref, sw_ref, sb_ref, wbr_ref, wout_ref, fg_ref,
                  o_ref,
                  h_scr, ext_scr, st_scr, vn_scr, vsave_scr, z_scr, ya_scr, *maybe_xt_scr,
                  first, final):
    i = pl.program_id(0)
    half = S5_BLOCK_STATE

    @pl.when(i == 0)
    def _():
        st_scr[...] = jnp.zeros_like(st_scr)
        ext_scr[0:HALO_ROWS, :] = jnp.zeros((HALO_ROWS, D_BRANCH), F32)
        vsave_scr[...] = jnp.zeros_like(vsave_scr)

    def loop_a(c):
        r = _rows(c)
        if first:
            xb = x_ref[:, _rows(c, n=T_CHUNK), :]
            xc = pltpu.einshape("btd->tbd", xb).reshape(ROW_CHUNK, D_MODEL)
            maybe_xt_scr[0][r, :] = xc
        else:
            xc = x_ref[r, :]
        ms = jnp.mean(xc * xc, axis=-1, keepdims=True)
        hc = (xc * lax.rsqrt(ms + RMS_EPS) * g_ref[...]).astype(BF16)
        h_scr[r, :] = hc

        ext_scr[_rows(c, base=HALO_ROWS), :] = _dot(hc, win_ref[:, OFF_B_VAL:OFF_B_VAL + D_BRANCH])

        v = _dot(hc, win_ref[:, OFF_C_V:OFF_C_V + D_BRANCH])
        mu = jnp.mean(v, axis=-1, keepdims=True)
        vc = v - mu
        var = jnp.mean(vc * vc, axis=-1, keepdims=True)
        vn = vc * lax.rsqrt(var + LN_EPS) * lng_ref[...] + lnb_ref[...]
        for hd in range(SGU_HEADS):
            vn_scr[hd, r, :] = vn[:, hd * SGU_HEAD_DIM:(hd + 1) * SGU_HEAD_DIM]

        aval = _dot(hc, win_ref[:, OFF_A_VAL:OFF_A_VAL + D_BRANCH])
        ys = []
        for j in range(S5_BLOCKS):
            bu = _dot(aval[:, j * LANES:(j + 1) * LANES].astype(BF16), wb_ref[j])
            a_re, a_im = are_ref[j], aim_ref[j]
            state = []
            for s in range(N_SUB):
                rs = slice(SUBLANES * s, SUBLANES * (s + 1))
                state += [st_scr[j, rs, 0:half], st_scr[j, rs, half:2 * half]]
            hs_rows = []
            for t in range(T_CHUNK):
                for s in range(N_SUB):
                    h_re, h_im = state[2 * s], state[2 * s + 1]
                    b = bu[t * BATCH + s * SUBLANES:t * BATCH + (s + 1) * SUBLANES, :]
                    state[2 * s] = a_re * h_re - a_im * h_im + b[:, :half]
                    state[2 * s + 1] = a_re * h_im + a_im * h_re + b[:, half:]
                    hs_rows.append(jnp.concatenate([state[2 * s], state[2 * s + 1]], axis=1))
            for s in range(N_SUB):
                rs = slice(SUBLANES * s, SUBLANES * (s + 1))
                st_scr[j, rs, 0:half] = state[2 * s]
                st_scr[j, rs, half:2 * half] = state[2 * s + 1]
            hs = jnp.concatenate(hs_rows, axis=0).astype(BF16)
            ys.append(_dot(hs, wc_ref[j]))
        y = jnp.concatenate(ys, axis=1) + dsk_ref[...] * aval
        y = jax.nn.gelu(y)
        y = y * jax.nn.sigmoid(_dot(y.astype(BF16), wglu_ref[...]) + bglu_ref[...])
        gate = _dot(hc, win_ref[:, OFF_A_GATE:OFF_A_GATE + D_BRANCH])
        ya_scr[r, :] = (y * jax.nn.silu(gate)).astype(BF16)
    _for_chunks(loop_a)

    second = (i % 2) == 1
    t0 = pl.multiple_of((i % 2) * CHUNK, CHUNK)
    for hd in range(SGU_HEADS):
        w_rows = sw_ref[hd, pl.ds(t0, CHUNK), :]
        w_lo, w_hi = w_rows[:, 0:CHUNK], w_rows[:, CHUNK:2 * CHUNK]
        w_cur = jnp.where(second, w_hi, w_lo).astype(BF16)
        w_prev = jnp.where(second, w_lo, 0.0).astype(BF16)
        v_cur = jnp.concatenate(
            [vn_scr[hd, pl.ds(b, CHUNK, stride=BATCH), :] for b in range(BATCH)], axis=1).astype(BF16)
        zt = _dot(w_cur, v_cur) + _dot(w_prev, vsave_scr[hd]) + sb_ref[hd, pl.ds(t0, CHUNK), :]
        vsave_scr[hd] = v_cur
        for b in range(BATCH):
            z_scr[hd, pl.ds(b, CHUNK, stride=BATCH), :] = zt[:, b * SGU_HEAD_DIM:(b + 1) * SGU_HEAD_DIM]

    def loop_b(c):
        r = _rows(c)
        hc = h_scr[r, :]

        t_idx = lax.broadcasted_iota(jnp.int32, (ROW_CHUNK, 1), 0) // BATCH
        pos = (i * T_TILE + c * T_CHUNK + 1 + t_idx).astype(F32)
        ys = []
        for gi, w in enumerate(POOL_WINDOWS):
            cols = slice(gi * POOL_GROUP, (gi + 1) * POOL_GROUP)
            cur = ext_scr[_rows(c, base=HALO_ROWS), cols]
            acc = cur
            for k in range(1, w):
                acc = acc + ext_scr[_rows(c, base=HALO_ROWS - k * BATCH), cols]
            p = acc / jnp.minimum(pos, float(w)) - cur
            ys.append(_dot(p.astype(BF16), pw_ref[gi]))
        y = jnp.concatenate(ys, axis=1) * psc_ref[...]
        gate = _dot(hc, win_ref[:, OFF_B_GATE:OFF_B_GATE + D_BRANCH])
        yb = (y * jax.nn.silu(gate)).astype(BF16)

        z = jnp.concatenate([z_scr[hd, r, :] for hd in range(SGU_HEADS)], axis=1)
        u = _dot(hc, win_ref[:, OFF_C_U:OFF_C_U + D_BRANCH])
        gate = _dot(hc, win_ref[:, OFF_C_GATE:OFF_C_GATE + D_BRANCH])
        yc = (u * z * jax.nn.silu(gate)).astype(BF16)

        merged = None
        for k, yk in enumerate((ya_scr[r, :], yb, yc)):
            mg = _dot(hc, win_ref[:, OFF_GATES + k * D_MODEL:OFF_GATES + (k + 1) * D_MODEL])
            term = jax.nn.sigmoid(mg) * _dot(yk, wbr_ref[k])
            merged = term if merged is None else merged + term
        x_res = maybe_xt_scr[0][r, :] if first else x_ref[r, :]
        out = x_res + _dot(merged.astype(BF16), wout_ref[...])
        if final:
            ms = jnp.mean(out * out, axis=-1, keepdims=True)
            out = out * lax.rsqrt(ms + RMS_EPS) * fg_ref[...]
            o_ref[:, _rows(c, n=T_CHUNK), :] = pltpu.einshape(
                "tbd->btd", out.reshape(T_CHUNK, BATCH, D_MODEL))
        else:
            o_ref[r, :] = out
    _for_chunks(loop_b)

    ext_scr[0:HALO_ROWS, :] = ext_scr[ROWS:ROWS + HALO_ROWS, :]


def _const_spec(shape):
    nd = len(shape)
    return pl.BlockSpec(shape, lambda i, _nd=nd: (0,) * _nd, pipeline_mode=pl.Buffered(1))


def _layer_call(x, params, first, final):
    batch_major = pl.BlockSpec((BATCH, T_TILE, D_MODEL), lambda i: (0, i, 0))
    time_major = pl.BlockSpec((ROWS, D_MODEL), lambda i: (i, 0))
    in_specs = [batch_major if first else time_major] + [_const_spec(p.shape) for p in params]
    scratch = [
        pltpu.VMEM((ROWS, D_MODEL), BF16),
        pltpu.VMEM((HALO_ROWS + ROWS, D_BRANCH), F32),
        pltpu.VMEM((S5_BLOCKS, BATCH, 2 * S5_BLOCK_STATE), F32),
        pltpu.VMEM((SGU_HEADS, ROWS, SGU_HEAD_DIM), F32),
        pltpu.VMEM((SGU_HEADS, CHUNK, BATCH * SGU_HEAD_DIM), BF16),
        pltpu.VMEM((SGU_HEADS, ROWS, SGU_HEAD_DIM), F32),
        pltpu.VMEM((ROWS, D_BRANCH), BF16),
    ]
    if first:
        scratch.append(pltpu.VMEM((ROWS, D_MODEL), F32))
    return pl.pallas_call(
        functools.partial(_layer_kernel, first=first, final=final),
        grid=(SEQ // T_TILE,),
        in_specs=in_specs,
        out_specs=batch_major if final else time_major,
        out_shape=jax.ShapeDtypeStruct((BATCH, SEQ, D_MODEL) if final else (SEQ * BATCH, D_MODEL), F32),
        scratch_shapes=scratch,
        compiler_params=pltpu.CompilerParams(
            dimension_semantics=("arbitrary",), vmem_limit_bytes=VMEM_LIMIT_BYTES),
        name="trunk_layer_final" if final else "trunk_layer",
    )(x, *params)


def _block_diag(blocks):
    n = blocks.shape[1]
    eye = jnp.eye(n, dtype=blocks.dtype)
    out = jnp.einsum('jgrc,gh->jgrhc', blocks, eye)
    return out.reshape(blocks.shape[0], n * blocks.shape[2], n * blocks.shape[3])


def _s5_params(lam_re, lam_im, log_dt, b_re, b_im, c_re, c_im):
    dt = jnp.exp(log_dt)[:, None]
    mag = jnp.exp(lam_re * dt)
    ab_re = mag * jnp.cos(lam_im * dt)
    ab_im = mag * jnp.sin(lam_im * dt)
    den = lam_re * lam_re + lam_im * lam_im
    nr = ab_re - 1.0
    ni = ab_im
    k_re = (nr * lam_re + ni * lam_im) / den
    k_im = (ni * lam_re - nr * lam_im) / den
    bb_re = k_re[..., None] * b_re - k_im[..., None] * b_im
    bb_im = k_re[..., None] * b_im + k_im[..., None] * b_re
    gpb = LANES // S5_GROUP

    def blk(a):
        return a.reshape((S5_BLOCKS, gpb) + a.shape[1:])

    def rep(a):
        return jnp.broadcast_to(a.reshape(S5_BLOCKS, 1, S5_BLOCK_STATE), (S5_BLOCKS, SUBLANES, S5_BLOCK_STATE))
    wb = jnp.concatenate([_block_diag(blk(bb_re.transpose(0, 2, 1))),
                          _block_diag(blk(bb_im.transpose(0, 2, 1)))], axis=2)
    wc = jnp.concatenate([_block_diag(blk(c_re.transpose(0, 2, 1))),
                          _block_diag(blk(-c_im.transpose(0, 2, 1)))], axis=1)
    return wb.astype(BF16), rep(ab_re), rep(ab_im), wc.astype(BF16)


def kernel(x, norm_g, w_in, s5_lam_re, s5_lam_im, s5_log_dt, s5_b_re, s5_b_im, s5_c_re, s5_c_im, s5_d, s5_w_glu, s5_b_glu, pool_w, pool_scale, sgu_ln_g, sgu_ln_b, sgu_w, sgu_b, w_branch, w_out, final_norm_g):
    bsz, seq, d = x.shape
    assert (bsz, seq, d) == (BATCH, SEQ, D_MODEL)
    for l in range(DEPTH):
        wb, a_re, a_im, wc = _s5_params(s5_lam_re[l], s5_lam_im[l], s5_log_dt[l], s5_b_re[l], s5_b_im[l],
                                        s5_c_re[l], s5_c_im[l])
        params = (
            norm_g[l].reshape(1, d), w_in[l].astype(BF16), wb, a_re, a_im, wc,
            s5_d[l].reshape(1, D_BRANCH), s5_w_glu[l].astype(BF16), s5_b_glu[l].reshape(1, D_BRANCH),
            pool_w[l].astype(BF16), pool_scale[l].reshape(1, D_BRANCH),
            sgu_ln_g[l].reshape(1, D_BRANCH), sgu_ln_b[l].reshape(1, D_BRANCH),
            sgu_w[l], sgu_b[l].reshape(SGU_HEADS, SGU_BLOCK, 1),
            w_branch[l].astype(BF16), w_out[l].astype(BF16), final_norm_g.reshape(1, d),
        )
        x = _layer_call(x, params, first=(l == 0), final=(l == DEPTH - 1))
    return x
```

```python
import functools

import jax
import jax.numpy as jnp
from jax import lax
from jax.experimental import pallas as pl
from jax.experimental.pallas import tpu as pltpu

D_MODEL = 1024
BATCH = 16
SEQ = 2048
DEPTH = 2
CHUNK = 64
D_BRANCH = 512
N_BRANCH = 3
S5_GROUP = 16
S5_GROUPS = D_BRANCH // S5_GROUP
S5_STATE = 64
POOL_WINDOWS = (2, 4, 8, 16)
POOL_GROUP = D_BRANCH // len(POOL_WINDOWS)
SGU_BLOCK = 128
SGU_HEADS = 4
SGU_HEAD_DIM = D_BRANCH // SGU_HEADS
RMS_EPS = 1e-6
LN_EPS = 1e-5

LANES = 128
SUBLANES = 8
S5_BLOCKS = D_BRANCH // LANES
S5_BLOCK_STATE = (LANES // S5_GROUP) * S5_STATE
T_TILE = CHUNK
ROWS = T_TILE * BATCH
ROW_CHUNK = 512
N_CHUNKS = ROWS // ROW_CHUNK
T_CHUNK = ROW_CHUNK // BATCH
HALO_ROWS = max(POOL_WINDOWS) * BATCH
N_SUB = BATCH // SUBLANES
S5_STEP = 2
VMEM_LIMIT_BYTES = 62 * 1024 * 1024

OFF_A_VAL, OFF_A_GATE, OFF_B_VAL, OFF_B_GATE, OFF_C_U, OFF_C_V, OFF_C_GATE, OFF_GATES = (
    0, 512, 1024, 1536, 2048, 2560, 3072, 3584)

F32 = jnp.float32
BF16 = jnp.bfloat16


def _rows(c, n=ROW_CHUNK, base=0):
    if isinstance(c, int):
        return pl.ds(base + c * n, n)
    return pl.ds(pl.multiple_of(base + c * n, SUBLANES), n)


def _z_row(c, t, bh):
    tb, tl = divmod(t, SUBLANES)
    static = (tb * BATCH + bh * SUBLANES) * SUBLANES + tl
    return c * (T_CHUNK * BATCH) + static


def _dot(a, b):
    return jnp.dot(a, b, preferred_element_type=F32)


def _for_chunks(body):
    def wrapped(c, carry):
        body(c)
        return carry
    lax.fori_loop(0, N_CHUNKS, wrapped, 0)


def _layer_kernel(x_ref, g_ref, win_ref, wk_ref, are_ref, aim_ref, wo_ref, tz_ref, dsk_ref, wglu_ref, bglu_ref,
                  pw_ref, psc_ref, lng_ref, lnb_ref, sw_ref, sb_ref, wbr_ref, wout_ref, fg_ref,
                  o_ref,
                  h_scr, ext_scr, st_scr, vn_scr, vsave_scr, z_scr, ya_scr, *maybe_xt_scr,
                  first, final):
    i = pl.program_id(0)
    half = S5_BLOCK_STATE

    @pl.when(i == 0)
    def _():
        st_scr[...] = jnp.zeros_like(st_scr)
        ext_scr[0:HALO_ROWS, :] = jnp.zeros((HALO_ROWS, D_BRANCH), F32)
        vsave_scr[...] = jnp.zeros_like(vsave_scr)

    def loop_a(c):
        r = _rows(c)
        if first:
            xb = x_ref[:, _rows(c, n=T_CHUNK), :]
            xc = jnp.swapaxes(xb, 0, 1).reshape(ROW_CHUNK, D_MODEL)
            maybe_xt_scr[0][r, :] = xc
        else:
            xc = x_ref[r, :]
        ms = jnp.mean(xc * xc, axis=-1, keepdims=True)
        hc = (xc * lax.rsqrt(ms + RMS_EPS) * g_ref[...]).astype(BF16)
        h_scr[r, :] = hc

        def project_pool_value():
            ext_scr[_rows(c, base=HALO_ROWS), :] = _dot(hc, win_ref[:, OFF_B_VAL:OFF_B_VAL + D_BRANCH])

        def project_sgu_value():
            v = _dot(hc, win_ref[:, OFF_C_V:OFF_C_V + D_BRANCH])
            mu = jnp.mean(v, axis=-1, keepdims=True)
            vc = v - mu
            var = jnp.mean(vc * vc, axis=-1, keepdims=True)
            vn = vc * lax.rsqrt(var + LN_EPS) * lng_ref[...] + lnb_ref[...]
            for hd in range(SGU_HEADS):
                for bh in range(N_SUB):
                    slab = jnp.concatenate(
                        [vn[t * BATCH + bh * SUBLANES:t * BATCH + (bh + 1) * SUBLANES,
                            hd * SGU_HEAD_DIM:(hd + 1) * SGU_HEAD_DIM] for t in range(T_CHUNK)], axis=0)
                    vn_scr[hd, _rows(c, n=T_CHUNK * SUBLANES, base=bh * T_TILE * SUBLANES), :] = slab

        gate = []

        def project_s5_gate():
            gate.append(_dot(hc, win_ref[:, OFF_A_GATE:OFF_A_GATE + D_BRANCH]))

        independent = [project_pool_value, project_sgu_value, project_s5_gate]

        aval = _dot(hc, win_ref[:, OFF_A_VAL:OFF_A_VAL + D_BRANCH])
        n_blk = T_CHUNK // S5_STEP
        ys = []
        for j in range(S5_BLOCKS):
            u = aval[:, j * LANES:(j + 1) * LANES].astype(BF16)
            ub = jnp.concatenate(
                [jnp.concatenate([u[(S5_STEP * blk + i) * BATCH:(S5_STEP * blk + i + 1) * BATCH, :]
                                  for blk in range(n_blk)], axis=0) for i in range(S5_STEP)], axis=1)
            cin = _dot(ub, wk_ref[j])
            if independent:
                independent.pop(0)()
            a_re, a_im = are_ref[j], aim_ref[j]
            state = []
            for s in range(N_SUB):
                rs = slice(SUBLANES * s, SUBLANES * (s + 1))
                state += [st_scr[j, rs, 0:half], st_scr[j, rs, half:2 * half]]
            hp_rows = []
            for blk in range(n_blk):
                for s in range(N_SUB):
                    h_re, h_im = state[2 * s], state[2 * s + 1]
                    hp_rows.append(jnp.concatenate([h_re, h_im], axis=1))
                    b = cin[blk * BATCH + s * SUBLANES:blk * BATCH + (s + 1) * SUBLANES, :]
                    state[2 * s] = a_re * h_re - a_im * h_im + b[:, :half]
                    state[2 * s + 1] = a_re * h_im + a_im * h_re + b[:, half:]
            for s in range(N_SUB):
                rs = slice(SUBLANES * s, SUBLANES * (s + 1))
                st_scr[j, rs, 0:half] = state[2 * s]
                st_scr[j, rs, half:2 * half] = state[2 * s + 1]
            hp = jnp.concatenate(hp_rows, axis=0).astype(BF16)
            yb = _dot(hp, wo_ref[j]) + _dot(ub, tz_ref[j])
            ys.append(jnp.concatenate(
                [yb[blk * BATCH:(blk + 1) * BATCH, i * LANES:(i + 1) * LANES]
                 for blk in range(n_blk) for i in range(S5_STEP)], axis=0))
        while independent:
            independent.pop(0)()
        y = jnp.concatenate(ys, axis=1) + dsk_ref[...] * aval
        y = jax.nn.gelu(y)
        y = y * jax.nn.sigmoid(_dot(y.astype(BF16), wglu_ref[...]) + bglu_ref[...])
        ya_scr[r, :] = (y * jax.nn.silu(gate[0])).astype(BF16)
    _for_chunks(loop_a)

    second = (i % 2) == 1
    t0 = pl.multiple_of((i % 2) * CHUNK, CHUNK)
    for hd in range(SGU_HEADS):
        w_rows = sw_ref[hd, pl.ds(t0, CHUNK), :]
        w_lo, w_hi = w_rows[:, 0:CHUNK], w_rows[:, CHUNK:2 * CHUNK]
        w_cur = jnp.where(second, w_hi, w_lo).astype(BF16)
        w_prev = jnp.where(second, w_lo, 0.0).astype(BF16)
        v_cur = jnp.concatenate(
            [vn_scr[hd, pl.ds((b // SUBLANES) * T_TILE * SUBLANES + b % SUBLANES, CHUNK, stride=SUBLANES), :]
             for b in range(BATCH)], axis=1).astype(BF16)
        zt = _dot(w_cur, v_cur) + _dot(w_prev, vsave_scr[hd]) + sb_ref[hd, pl.ds(t0, CHUNK), :]
        vsave_scr[hd] = v_cur
        for tb in range(CHUNK // SUBLANES):
            z_scr[hd, tb * BATCH * SUBLANES:(tb + 1) * BATCH * SUBLANES, :] = jnp.concatenate(
                [zt[tb * SUBLANES:(tb + 1) * SUBLANES, b * SGU_HEAD_DIM:(b + 1) * SGU_HEAD_DIM]
                 for b in range(BATCH)], axis=0)

    def loop_b(c):
        r = _rows(c)
        hc = h_scr[r, :]

        t_idx = lax.broadcasted_iota(jnp.int32, (ROW_CHUNK, 1), 0) // BATCH
        pos = (i * T_TILE + c * T_CHUNK + 1 + t_idx).astype(F32)
        ys = []
        for gi, w in enumerate(POOL_WINDOWS):
            cols = slice(gi * POOL_GROUP, (gi + 1) * POOL_GROUP)
            cur = ext_scr[_rows(c, base=HALO_ROWS), cols]
            acc = cur
            for k in range(1, w):
                acc = acc + ext_scr[_rows(c, base=HALO_ROWS - k * BATCH), cols]
            p = acc / jnp.minimum(pos, float(w)) - cur
            ys.append(_dot(p.astype(BF16), pw_ref[gi]))
        y = jnp.concatenate(ys, axis=1) * psc_ref[...]
        gate = _dot(hc, win_ref[:, OFF_B_GATE:OFF_B_GATE + D_BRANCH])
        yb = (y * jax.nn.silu(gate)).astype(BF16)

        z = jnp.concatenate(
            [jnp.concatenate(
                [z_scr[hd, pl.ds(_z_row(c, t, bh), SUBLANES, stride=SUBLANES), :]
                 for t in range(T_CHUNK) for bh in range(N_SUB)], axis=0)
             for hd in range(SGU_HEADS)], axis=1)
        u = _dot(hc, win_ref[:, OFF_C_U:OFF_C_U + D_BRANCH])
        gate = _dot(hc, win_ref[:, OFF_C_GATE:OFF_C_GATE + D_BRANCH])
        yc = (u * z * jax.nn.silu(gate)).astype(BF16)

        merged = None
        for k, yk in enumerate((ya_scr[r, :], yb, yc)):
            mg = _dot(hc, win_ref[:, OFF_GATES + k * D_MODEL:OFF_GATES + (k + 1) * D_MODEL])
            term = jax.nn.sigmoid(mg) * _dot(yk, wbr_ref[k])
            merged = term if merged is None else merged + term
        x_res = maybe_xt_scr[0][r, :] if first else x_ref[r, :]
        out = x_res + _dot(merged.astype(BF16), wout_ref[...])
        if final:
            ms = jnp.mean(out * out, axis=-1, keepdims=True)
            out = out * lax.rsqrt(ms + RMS_EPS) * fg_ref[...]
            o_ref[:, _rows(c, n=T_CHUNK), :] = jnp.swapaxes(out.reshape(T_CHUNK, BATCH, D_MODEL), 0, 1)
        else:
            o_ref[r, :] = out
    _for_chunks(loop_b)

    ext_scr[0:HALO_ROWS, :] = ext_scr[ROWS:ROWS + HALO_ROWS, :]


def _const_spec(shape):
    nd = len(shape)
    return pl.BlockSpec(shape, lambda i, _nd=nd: (0,) * _nd, pipeline_mode=pl.Buffered(1))


def _layer_call(x, params, first, final):
    batch_major = pl.BlockSpec((BATCH, T_TILE, D_MODEL), lambda i: (0, i, 0))
    time_major = pl.BlockSpec((ROWS, D_MODEL), lambda i: (i, 0))
    in_specs = [batch_major if first else time_major] + [_const_spec(p.shape) for p in params]
    scratch = [
        pltpu.VMEM((ROWS, D_MODEL), BF16),
        pltpu.VMEM((HALO_ROWS + ROWS, D_BRANCH), F32),
        pltpu.VMEM((S5_BLOCKS, BATCH, 2 * S5_BLOCK_STATE), F32),
        pltpu.VMEM((SGU_HEADS, ROWS, SGU_HEAD_DIM), F32),
        pltpu.VMEM((SGU_HEADS, CHUNK, BATCH * SGU_HEAD_DIM), BF16),
        pltpu.VMEM((SGU_HEADS, ROWS, SGU_HEAD_DIM), F32),
        pltpu.VMEM((ROWS, D_BRANCH), BF16),
    ]
    if first:
        scratch.append(pltpu.VMEM((ROWS, D_MODEL), F32))
    return pl.pallas_call(
        functools.partial(_layer_kernel, first=first, final=final),
        grid=(SEQ // T_TILE,),
        in_specs=in_specs,
        out_specs=batch_major if final else time_major,
        out_shape=jax.ShapeDtypeStruct((BATCH, SEQ, D_MODEL) if final else (SEQ * BATCH, D_MODEL), F32),
        scratch_shapes=scratch,
        compiler_params=pltpu.CompilerParams(
            dimension_semantics=("arbitrary",), vmem_limit_bytes=VMEM_LIMIT_BYTES),
        name="trunk_layer_final" if final else "trunk_layer",
    )(x, *params)


def _s5_params(lam_re, lam_im, log_dt, b_re, b_im, c_re, c_im):
    hi = lax.Precision.HIGHEST
    dt = jnp.exp(log_dt)[:, None]
    mag = jnp.exp(lam_re * dt)
    ab_re = mag * jnp.cos(lam_im * dt)
    ab_im = mag * jnp.sin(lam_im * dt)
    den = lam_re * lam_re + lam_im * lam_im
    nr = ab_re - 1.0
    ni = ab_im
    k_re = (nr * lam_re + ni * lam_im) / den
    k_im = (ni * lam_re - nr * lam_im) / den
    bb_re = k_re[..., None] * b_re - k_im[..., None] * b_im
    bb_im = k_re[..., None] * b_im + k_im[..., None] * b_re
    gpb = LANES // S5_GROUP

    pw = [(jnp.ones_like(ab_re), jnp.zeros_like(ab_im))]
    for _ in range(S5_STEP):
        p_re, p_im = pw[-1]
        pw.append((p_re * ab_re - p_im * ab_im, p_re * ab_im + p_im * ab_re))

    def blocks(a):
        a = a.reshape((S5_BLOCKS, gpb) + a.shape[1:])
        eye = jnp.eye(gpb, dtype=a.dtype)
        out = jnp.einsum('jgrc,gh->jgrhc', a, eye)
        return out.reshape(S5_BLOCKS, gpb * a.shape[2], gpb * a.shape[3])

    def rep(a):
        return jnp.broadcast_to(a.reshape(S5_BLOCKS, 1, S5_BLOCK_STATE), (S5_BLOCKS, SUBLANES, S5_BLOCK_STATE))

    wk_rows, wo_cols, tz_rows = [], [], []
    for i in range(S5_STEP):
        p_re, p_im = pw[S5_STEP - 1 - i]
        ab_b_re = (p_re[..., None] * bb_re - p_im[..., None] * bb_im).transpose(0, 2, 1)
        ab_b_im = (p_re[..., None] * bb_im + p_im[..., None] * bb_re).transpose(0, 2, 1)
        wk_rows.append(jnp.concatenate([blocks(ab_b_re), blocks(ab_b_im)], axis=2))
        p_re, p_im = pw[i + 1]
        ca_re = (c_re * p_re[:, None, :] - c_im * p_im[:, None, :]).transpose(0, 2, 1)
        ca_im = (c_re * p_im[:, None, :] + c_im * p_re[:, None, :]).transpose(0, 2, 1)
        wo_cols.append(jnp.concatenate([blocks(ca_re), blocks(-ca_im)], axis=1))
    for m in range(S5_STEP):
        row = []
        for i in range(S5_STEP):
            if m > i:
                row.append(jnp.zeros((S5_BLOCKS, LANES, LANES), jnp.float32))
                continue
            p_re, p_im = pw[i - m]
            ca_re = c_re * p_re[:, None, :] - c_im * p_im[:, None, :]
            ca_im = c_re * p_im[:, None, :] + c_im * p_re[:, None, :]
            t = (jnp.einsum('gop,gpc->gco', ca_re, bb_re, precision=hi)
                 - jnp.einsum('gop,gpc->gco', ca_im, bb_im, precision=hi))
            row.append(blocks(t))
        tz_rows.append(jnp.concatenate(row, axis=2))
    wk = jnp.concatenate(wk_rows, axis=1)
    wo = jnp.concatenate(wo_cols, axis=2)
    tz = jnp.concatenate(tz_rows, axis=1)
    a_re, a_im = pw[S5_STEP]
    return wk.astype(BF16), rep(a_re), rep(a_im), wo.astype(BF16), tz.astype(BF16)


def kernel(x, norm_g, w_in, s5_lam_re, s5_lam_im, s5_log_dt, s5_b_re, s5_b_im, s5_c_re, s5_c_im, s5_d, s5_w_glu, s5_b_glu, pool_w, pool_scale, sgu_ln_g, sgu_ln_b, sgu_w, sgu_b, w_branch, w_out, final_norm_g):
    bsz, seq, d = x.shape
    assert (bsz, seq, d) == (BATCH, SEQ, D_MODEL)
    for l in range(DEPTH):
        wk, a_re, a_im, wo, tz = _s5_params(s5_lam_re[l], s5_lam_im[l], s5_log_dt[l], s5_b_re[l], s5_b_im[l],
                                            s5_c_re[l], s5_c_im[l])
        params = (
            norm_g[l].reshape(1, d), w_in[l].astype(BF16), wk, a_re, a_im, wo, tz,
            s5_d[l].reshape(1, D_BRANCH), s5_w_glu[l].astype(BF16), s5_b_glu[l].reshape(1, D_BRANCH),
            pool_w[l].astype(BF16), pool_scale[l].reshape(1, D_BRANCH),
            sgu_ln_g[l].reshape(1, D_BRANCH), sgu_ln_b[l].reshape(1, D_BRANCH),
            sgu_w[l], sgu_b[l].reshape(SGU_HEADS, SGU_BLOCK, 1),
            w_branch[l].astype(BF16), w_out[l].astype(BF16), final_norm_g.reshape(1, d),
        )
        x = _layer_call(x, params, first=(l == 0), final=(l == DEPTH - 1))
    return x
```

```python
import functools

import jax
import jax.numpy as jnp
from jax import lax
from jax.experimental import pallas as pl
from jax.experimental.pallas import tpu as pltpu

D_MODEL = 1024
BATCH = 16
SEQ = 2048
DEPTH = 2
CHUNK = 64
D_BRANCH = 512
N_BRANCH = 3
S5_GROUP = 16
S5_GROUPS = D_BRANCH // S5_GROUP
S5_STATE = 64
POOL_WINDOWS = (2, 4, 8, 16)
POOL_GROUP = D_BRANCH // len(POOL_WINDOWS)
SGU_BLOCK = 128
SGU_HEADS = 4
SGU_HEAD_DIM = D_BRANCH // SGU_HEADS
RMS_EPS = 1e-6
LN_EPS = 1e-5

LANES = 128
SUBLANES = 8
S5_BLOCKS = D_BRANCH // LANES
S5_BLOCK_STATE = (LANES // S5_GROUP) * S5_STATE
T_TILE = CHUNK
ROWS = T_TILE * BATCH
ROW_CHUNK = 512
N_CHUNKS = ROWS // ROW_CHUNK
T_CHUNK = ROW_CHUNK // BATCH
HALO_ROWS = max(POOL_WINDOWS) * BATCH
N_SUB = BATCH // SUBLANES
S5_STEP = 2
VMEM_LIMIT_BYTES = 62 * 1024 * 1024

OFF_A_VAL, OFF_A_GATE, OFF_B_VAL, OFF_B_GATE, OFF_C_U, OFF_C_V, OFF_C_GATE, OFF_GATES = (
    0, 512, 1024, 1536, 2048, 2560, 3072, 3584)

F32 = jnp.float32
BF16 = jnp.bfloat16


def _rows(c, n=ROW_CHUNK, base=0):
    if isinstance(c, int):
        return pl.ds(base + c * n, n)
    return pl.ds(pl.multiple_of(base + c * n, SUBLANES), n)


def _z_row(c, t, bh):
    tb, tl = divmod(t, SUBLANES)
    static = (tb * BATCH + bh * SUBLANES) * SUBLANES + tl
    return c * (T_CHUNK * BATCH) + static


def _dot(a, b):
    return jnp.dot(a, b, preferred_element_type=F32)


def _for_chunks(body):
    def wrapped(c, carry):
        body(c)
        return carry
    lax.fori_loop(0, N_CHUNKS, wrapped, 0)


def _layer_kernel(x_ref, g_ref, win_ref, wk_ref, are_ref, aim_ref, wo_ref, tz_ref, dsk_ref, wglu_ref, bglu_ref,
                  pw_ref, psc_ref, lng_ref, lnb_ref, sw_ref, sb_ref, wbr_ref, wout_ref, fg_ref,
                  o_ref,
                  h_scr, ext_scr, st_scr, vn_scr, vsave_scr, z_scr, ya_scr, *maybe_xt_scr,
                  first, final):
    i = pl.program_id(0)
    half = S5_BLOCK_STATE

    @pl.when(i == 0)
    def _():
        st_scr[...] = jnp.zeros_like(st_scr)
        ext_scr[0:HALO_ROWS, :] = jnp.zeros((HALO_ROWS, D_BRANCH), F32)
        vsave_scr[...] = jnp.zeros_like(vsave_scr)

    def loop_a(c):
        r = _rows(c)
        if first:
            xb = x_ref[:, _rows(c, n=T_CHUNK), :]
            xc = jnp.swapaxes(xb, 0, 1).reshape(ROW_CHUNK, D_MODEL)
            maybe_xt_scr[0][r, :] = xc
        else:
            xc = x_ref[r, :]
        ms = jnp.mean(xc * xc, axis=-1, keepdims=True)
        hc = (xc * lax.rsqrt(ms + RMS_EPS) * g_ref[...]).astype(BF16)
        h_scr[r, :] = hc

        def project_pool_value():
            ext_scr[_rows(c, base=HALO_ROWS), :] = _dot(hc, win_ref[:, OFF_B_VAL:OFF_B_VAL + D_BRANCH])

        def project_sgu_value():
            v = _dot(hc, win_ref[:, OFF_C_V:OFF_C_V + D_BRANCH])
            mu = jnp.mean(v, axis=-1, keepdims=True)
            vc = v - mu
            var = jnp.mean(vc * vc, axis=-1, keepdims=True)
            vn = vc * lax.rsqrt(var + LN_EPS) * lng_ref[...] + lnb_ref[...]
            for hd in range(SGU_HEADS):
                for bh in range(N_SUB):
                    slab = jnp.concatenate(
                        [vn[t * BATCH + bh * SUBLANES:t * BATCH + (bh + 1) * SUBLANES,
                            hd * SGU_HEAD_DIM:(hd + 1) * SGU_HEAD_DIM] for t in range(T_CHUNK)], axis=0)
                    vn_scr[hd, _rows(c, n=T_CHUNK * SUBLANES, base=bh * T_TILE * SUBLANES), :] = slab

        gate = []

        def project_s5_gate():
            gate.append(_dot(hc, win_ref[:, OFF_A_GATE:OFF_A_GATE + D_BRANCH]))

        independent = [project_pool_value, project_sgu_value, project_s5_gate]

        aval = _dot(hc, win_ref[:, OFF_A_VAL:OFF_A_VAL + D_BRANCH])
        n_blk = T_CHUNK // S5_STEP
        ys = []
        for j in range(S5_BLOCKS):
            u = aval[:, j * LANES:(j + 1) * LANES].astype(BF16)
            ub = jnp.concatenate(
                [jnp.concatenate([u[(S5_STEP * blk + i) * BATCH:(S5_STEP * blk + i + 1) * BATCH, :]
                                  for blk in range(n_blk)], axis=0) for i in range(S5_STEP)], axis=1)
            cin = _dot(ub, wk_ref[j])
            if independent:
                independent.pop(0)()
            a_re, a_im = are_ref[j], aim_ref[j]
            state = []
            for s in range(N_SUB):
                rs = slice(SUBLANES * s, SUBLANES * (s + 1))
                state += [st_scr[j, rs, 0:half], st_scr[j, rs, half:2 * half]]
            hp_rows = []
            for blk in range(n_blk):
                for s in range(N_SUB):
                    h_re, h_im = state[2 * s], state[2 * s + 1]
                    hp_rows.append(jnp.concatenate([h_re, h_im], axis=1))
                    b = cin[blk * BATCH + s * SUBLANES:blk * BATCH + (s + 1) * SUBLANES, :]
                    state[2 * s] = a_re * h_re - a_im * h_im + b[:, :half]
                    state[2 * s + 1] = a_re * h_im + a_im * h_re + b[:, half:]
            for s in range(N_SUB):
                rs = slice(SUBLANES * s, SUBLANES * (s + 1))
                st_scr[j, rs, 0:half] = state[2 * s]
                st_scr[j, rs, half:2 * half] = state[2 * s + 1]
            hp = jnp.concatenate(hp_rows, axis=0).astype(BF16)
            yb = _dot(hp, wo_ref[j]) + _dot(ub, tz_ref[j])
            ys.append(jnp.concatenate(
                [yb[blk * BATCH:(blk + 1) * BATCH, i * LANES:(i + 1) * LANES]
                 for blk in range(n_blk) for i in range(S5_STEP)], axis=0))
        while independent:
            independent.pop(0)()
        y = jnp.concatenate(ys, axis=1) + dsk_ref[...] * aval
        y = jax.nn.gelu(y)
        y = y * jax.nn.sigmoid(_dot(y.astype(BF16), wglu_ref[...]) + bglu_ref[...])
        ya_scr[r, :] = (y * jax.nn.silu(gate[0])).astype(BF16)
    _for_chunks(loop_a)

    second = (i % 2) == 1
    t0 = pl.multiple_of((i % 2) * CHUNK, CHUNK)
    for hd in range(SGU_HEADS):
        w_rows = sw_ref[hd, pl.ds(t0, CHUNK), :]
        w_lo, w_hi = w_rows[:, 0:CHUNK], w_rows[:, CHUNK:2 * CHUNK]
        w_cur = jnp.where(second, w_hi, w_lo).astype(BF16)
        w_prev = jnp.where(second, w_lo, 0.0).astype(BF16)
        v_cur = jnp.concatenate(
            [vn_scr[hd, pl.ds((b // SUBLANES) * T_TILE * SUBLANES + b % SUBLANES, CHUNK, stride=SUBLANES), :]
             for b in range(BATCH)], axis=1).astype(BF16)
        zt = _dot(w_cur, v_cur) + _dot(w_prev, vsave_scr[hd]) + sb_ref[hd, pl.ds(t0, CHUNK), :]
        vsave_scr[hd] = v_cur
        for tb in range(CHUNK // SUBLANES):
            z_scr[hd, tb * BATCH * SUBLANES:(tb + 1) * BATCH * SUBLANES, :] = jnp.concatenate(
                [zt[tb * SUBLANES:(tb + 1) * SUBLANES, b * SGU_HEAD_DIM:(b + 1) * SGU_HEAD_DIM]
                 for b in range(BATCH)], axis=0)

    def loop_b(c):
        r = _rows(c)
        hc = h_scr[r, :]

        t_idx = lax.broadcasted_iota(jnp.int32, (ROW_CHUNK, 1), 0) // BATCH
        pos = (i * T_TILE + c * T_CHUNK + 1 + t_idx).astype(F32)
        ys = []
        for gi, w in enumerate(POOL_WINDOWS):
            cols = slice(gi * POOL_GROUP, (gi + 1) * POOL_GROUP)
            cur = ext_scr[_rows(c, base=HALO_ROWS), cols]
            acc = cur
            for k in range(1, w):
                acc = acc + ext_scr[_rows(c, base=HALO_ROWS - k * BATCH), cols]
            p = acc / jnp.minimum(pos, float(w)) - cur
            ys.append(_dot(p.astype(BF16), pw_ref[gi]))
        y = jnp.concatenate(ys, axis=1) * psc_ref[...]
        gate = _dot(hc, win_ref[:, OFF_B_GATE:OFF_B_GATE + D_BRANCH])
        yb = (y * jax.nn.silu(gate)).astype(BF16)

        z = jnp.concatenate(
            [jnp.concatenate(
                [z_scr[hd, pl.ds(_z_row(c, t, bh), SUBLANES, stride=SUBLANES), :]
                 for t in range(T_CHUNK) for bh in range(N_SUB)], axis=0)
             for hd in range(SGU_HEADS)], axis=1)
        u = _dot(hc, win_ref[:, OFF_C_U:OFF_C_U + D_BRANCH])
        gate = _dot(hc, win_ref[:, OFF_C_GATE:OFF_C_GATE + D_BRANCH])
        yc = (u * z * jax.nn.silu(gate)).astype(BF16)

        merged = None
        for k, yk in enumerate((ya_scr[r, :], yb, yc)):
            mg = _dot(hc, win_ref[:, OFF_GATES + k * D_MODEL:OFF_GATES + (k + 1) * D_MODEL])
            term = jax.nn.sigmoid(mg) * _dot(yk, wbr_ref[k])
            merged = term if merged is None else merged + term
        x_res = maybe_xt_scr[0][r, :] if first else x_ref[r, :]
        out = x_res + _dot(merged.astype(BF16), wout_ref[...])
        if final:
            ms = jnp.mean(out * out, axis=-1, keepdims=True)
            out = out * lax.rsqrt(ms + RMS_EPS) * fg_ref[...]
            o_ref[:, _rows(c, n=T_CHUNK), :] = jnp.swapaxes(out.reshape(T_CHUNK, BATCH, D_MODEL), 0, 1)
        else:
            o_ref[r, :] = out
    _for_chunks(loop_b)

    ext_scr[0:HALO_ROWS, :] = ext_scr[ROWS:ROWS + HALO_ROWS, :]


def _layer_spec(shape, layer):
    nd = len(shape)
    return pl.BlockSpec((None,) + tuple(shape[1:]), lambda i, _l=layer, _nd=nd: (_l,) + (0,) * (_nd - 1),
                        pipeline_mode=pl.Buffered(1))


def _layer_call(x, params, layer, first, final):
    batch_major = pl.BlockSpec((BATCH, T_TILE, D_MODEL), lambda i: (0, i, 0))
    time_major = pl.BlockSpec((ROWS, D_MODEL), lambda i: (i, 0))
    in_specs = [batch_major if first else time_major] + [
        _layer_spec(p.shape, min(layer, p.shape[0] - 1)) for p in params]
    scratch = [
        pltpu.VMEM((ROWS, D_MODEL), BF16),
        pltpu.VMEM((HALO_ROWS + ROWS, D_BRANCH), F32),
        pltpu.VMEM((S5_BLOCKS, BATCH, 2 * S5_BLOCK_STATE), F32),
        pltpu.VMEM((SGU_HEADS, ROWS, SGU_HEAD_DIM), F32),
        pltpu.VMEM((SGU_HEADS, CHUNK, BATCH * SGU_HEAD_DIM), BF16),
        pltpu.VMEM((SGU_HEADS, ROWS, SGU_HEAD_DIM), F32),
        pltpu.VMEM((ROWS, D_BRANCH), BF16),
    ]
    if first:
        scratch.append(pltpu.VMEM((ROWS, D_MODEL), F32))
    return pl.pallas_call(
        functools.partial(_layer_kernel, first=first, final=final),
        grid=(SEQ // T_TILE,),
        in_specs=in_specs,
        out_specs=batch_major if final else time_major,
        out_shape=jax.ShapeDtypeStruct((BATCH, SEQ, D_MODEL) if final else (SEQ * BATCH, D_MODEL), F32),
        scratch_shapes=scratch,
        compiler_params=pltpu.CompilerParams(
            dimension_semantics=("arbitrary",), vmem_limit_bytes=VMEM_LIMIT_BYTES),
        name="trunk_layer_final" if final else "trunk_layer",
    )(x, *params)


def _s5_params(lam_re, lam_im, log_dt, b_re, b_im, c_re, c_im):
    hi = lax.Precision.HIGHEST
    dt = jnp.exp(log_dt)[:, None]
    mag = jnp.exp(lam_re * dt)
    ab_re = mag * jnp.cos(lam_im * dt)
    ab_im = mag * jnp.sin(lam_im * dt)
    den = lam_re * lam_re + lam_im * lam_im
    nr = ab_re - 1.0
    ni = ab_im
    k_re = (nr * lam_re + ni * lam_im) / den
    k_im = (ni * lam_re - nr * lam_im) / den
    bb_re = k_re[..., None] * b_re - k_im[..., None] * b_im
    bb_im = k_re[..., None] * b_im + k_im[..., None] * b_re
    gpb = LANES // S5_GROUP

    pw = [(jnp.ones_like(ab_re), jnp.zeros_like(ab_im))]
    for _ in range(S5_STEP):
        p_re, p_im = pw[-1]
        pw.append((p_re * ab_re - p_im * ab_im, p_re * ab_im + p_im * ab_re))

    def blocks(a):
        a = a.reshape((S5_BLOCKS, gpb) + a.shape[1:])
        eye = jnp.eye(gpb, dtype=a.dtype)
        out = jnp.einsum('jgrc,gh->jgrhc', a, eye)
        return out.reshape(S5_BLOCKS, gpb * a.shape[2], gpb * a.shape[3])

    def rep(a):
        return jnp.broadcast_to(a.reshape(S5_BLOCKS, 1, S5_BLOCK_STATE), (S5_BLOCKS, SUBLANES, S5_BLOCK_STATE))

    wk_rows, wo_cols, tz_rows = [], [], []
    for i in range(S5_STEP):
        p_re, p_im = pw[S5_STEP - 1 - i]
        ab_b_re = (p_re[..., None] * bb_re - p_im[..., None] * bb_im).transpose(0, 2, 1)
        ab_b_im = (p_re[..., None] * bb_im + p_im[..., None] * bb_re).transpose(0, 2, 1)
        wk_rows.append(jnp.concatenate([blocks(ab_b_re), blocks(ab_b_im)], axis=2))
        p_re, p_im = pw[i + 1]
        ca_re = (c_re * p_re[:, None, :] - c_im * p_im[:, None, :]).transpose(0, 2, 1)
        ca_im = (c_re * p_im[:, None, :] + c_im * p_re[:, None, :]).transpose(0, 2, 1)
        wo_cols.append(jnp.concatenate([blocks(ca_re), blocks(-ca_im)], axis=1))
    for m in range(S5_STEP):
        row = []
        for i in range(S5_STEP):
            if m > i:
                row.append(jnp.zeros((S5_BLOCKS, LANES, LANES), jnp.float32))
                continue
            p_re, p_im = pw[i - m]
            ca_re = c_re * p_re[:, None, :] - c_im * p_im[:, None, :]
            ca_im = c_re * p_im[:, None, :] + c_im * p_re[:, None, :]
            t = (jnp.einsum('gop,gpc->gco', ca_re, bb_re, precision=hi)
                 - jnp.einsum('gop,gpc->gco', ca_im, bb_im, precision=hi))
            row.append(blocks(t))
        tz_rows.append(jnp.concatenate(row, axis=2))
    wk = jnp.concatenate(wk_rows, axis=1)
    wo = jnp.concatenate(wo_cols, axis=2)
    tz = jnp.concatenate(tz_rows, axis=1)
    a_re, a_im = pw[S5_STEP]
    return wk.astype(BF16), rep(a_re), rep(a_im), wo.astype(BF16), tz.astype(BF16)


def kernel(x, norm_g, w_in, s5_lam_re, s5_lam_im, s5_log_dt, s5_b_re, s5_b_im, s5_c_re, s5_c_im, s5_d, s5_w_glu, s5_b_glu, pool_w, pool_scale, sgu_ln_g, sgu_ln_b, sgu_w, sgu_b, w_branch, w_out, final_norm_g):
    bsz, seq, d = x.shape
    assert (bsz, seq, d) == (BATCH, SEQ, D_MODEL)
    wk, a_re, a_im, wo, tz = jax.vmap(_s5_params)(s5_lam_re, s5_lam_im, s5_log_dt, s5_b_re, s5_b_im,
                                                  s5_c_re, s5_c_im)
    params = (
        norm_g.reshape(DEPTH, 1, d), w_in.astype(BF16), wk, a_re, a_im, wo, tz,
        s5_d.reshape(DEPTH, 1, D_BRANCH), s5_w_glu.astype(BF16), s5_b_glu.reshape(DEPTH, 1, D_BRANCH),
        pool_w.astype(BF16), pool_scale.reshape(DEPTH, 1, D_BRANCH),
        sgu_ln_g.reshape(DEPTH, 1, D_BRANCH), sgu_ln_b.reshape(DEPTH, 1, D_BRANCH),
        sgu_w, sgu_b.reshape(DEPTH, SGU_HEADS, SGU_BLOCK, 1),
        w_branch.astype(BF16), w_out.astype(BF16), final_norm_g.reshape(1, 1, d),
    )
    for l in range(DEPTH):
        x = _layer_call(x, params, l, first=(l == 0), final=(l == DEPTH - 1))
    return x
```

```python
import functools

import jax
import jax.numpy as jnp
from jax import lax
from jax.experimental import pallas as pl
from jax.experimental.pallas import tpu as pltpu

D_MODEL = 1024
BATCH = 16
SEQ = 2048
DEPTH = 2
CHUNK = 64
D_BRANCH = 512
N_BRANCH = 3
S5_GROUP = 16
S5_GROUPS = D_BRANCH // S5_GROUP
S5_STATE = 64
POOL_WINDOWS = (2, 4, 8, 16)
POOL_GROUP = D_BRANCH // len(POOL_WINDOWS)
SGU_BLOCK = 128
SGU_HEADS = 4
SGU_HEAD_DIM = D_BRANCH // SGU_HEADS
RMS_EPS = 1e-6
LN_EPS = 1e-5

LANES = 128
SUBLANES = 8
S5_BLOCKS = D_BRANCH // LANES
S5_BLOCK_STATE = (LANES // S5_GROUP) * S5_STATE
T_TILE = CHUNK
ROWS = T_TILE * BATCH
ROW_CHUNK = 512
N_CHUNKS = ROWS // ROW_CHUNK
T_CHUNK = ROW_CHUNK // BATCH
HALO_ROWS = max(POOL_WINDOWS) * BATCH
N_SUB = BATCH // SUBLANES
S5_STEP = 2
VMEM_LIMIT_BYTES = 62 * 1024 * 1024

OFF_A_VAL, OFF_A_GATE, OFF_B_VAL, OFF_B_GATE, OFF_C_U, OFF_C_V, OFF_C_GATE, OFF_GATES = (
    0, 512, 1024, 1536, 2048, 2560, 3072, 3584)

F32 = jnp.float32
BF16 = jnp.bfloat16


def _rows(c, n=ROW_CHUNK, base=0):
    if isinstance(c, int):
        return pl.ds(base + c * n, n)
    return pl.ds(pl.multiple_of(base + c * n, SUBLANES), n)


def _z_row(c, t, bh):
    tb, tl = divmod(t, SUBLANES)
    static = (tb * BATCH + bh * SUBLANES) * SUBLANES + tl
    return c * (T_CHUNK * BATCH) + static


def _dot(a, b):
    return jnp.dot(a, b, preferred_element_type=F32)


def _for_chunks(body):
    for c in range(N_CHUNKS):
        body(c)


def _layer_kernel(x_ref, g_ref, win_ref, wk_ref, are_ref, aim_ref, wo_ref, tz_ref, dsk_ref, wglu_ref, bglu_ref,
                  pw_ref, psc_ref, lng_ref, lnb_ref, sw_ref, sb_ref, wbr_ref, wout_ref, fg_ref,
                  o_ref,
                  h_scr, ext_scr, st_scr, vn_scr, vsave_scr, z_scr, ya_scr,
                  *, first, final):
    i = pl.program_id(0)
    half = S5_BLOCK_STATE

    @pl.when(i == 0)
    def _():
        st_scr[...] = jnp.zeros_like(st_scr)
        ext_scr[0:HALO_ROWS, :] = jnp.zeros((HALO_ROWS, D_BRANCH), F32)
        vsave_scr[...] = jnp.zeros_like(vsave_scr)

    def loop_a(c):
        r = _rows(c)
        if first:
            xb = x_ref[:, _rows(c, n=T_CHUNK), :]
            xc = jnp.swapaxes(xb, 0, 1).reshape(ROW_CHUNK, D_MODEL)
            o_ref[r, :] = xc
        else:
            xc = x_ref[r, :]
        ms = jnp.mean(xc * xc, axis=-1, keepdims=True)
        hc = (xc * lax.rsqrt(ms + RMS_EPS) * g_ref[...]).astype(BF16)
        h_scr[r, :] = hc

        def project_pool_value():
            ext_scr[_rows(c, base=HALO_ROWS), :] = _dot(hc, win_ref[:, OFF_B_VAL:OFF_B_VAL + D_BRANCH])

        def project_sgu_value():
            v = _dot(hc, win_ref[:, OFF_C_V:OFF_C_V + D_BRANCH])
            mu = jnp.mean(v, axis=-1, keepdims=True)
            vc = v - mu
            var = jnp.mean(vc * vc, axis=-1, keepdims=True)
            vn = vc * lax.rsqrt(var + LN_EPS) * lng_ref[...] + lnb_ref[...]
            for hd in range(SGU_HEADS):
                for bh in range(N_SUB):
                    slab = jnp.concatenate(
                        [vn[t * BATCH + bh * SUBLANES:t * BATCH + (bh + 1) * SUBLANES,
                            hd * SGU_HEAD_DIM:(hd + 1) * SGU_HEAD_DIM] for t in range(T_CHUNK)], axis=0)
                    vn_scr[hd, _rows(c, n=T_CHUNK * SUBLANES, base=bh * T_TILE * SUBLANES), :] = slab

        gate = []

        def project_s5_gate():
            gate.append(_dot(hc, win_ref[:, OFF_A_GATE:OFF_A_GATE + D_BRANCH]))

        independent = [project_pool_value, project_sgu_value, project_s5_gate]

        aval = _dot(hc, win_ref[:, OFF_A_VAL:OFF_A_VAL + D_BRANCH])
        n_blk = T_CHUNK // S5_STEP
        ys = []
        for j in range(S5_BLOCKS):
            u = aval[:, j * LANES:(j + 1) * LANES].astype(BF16)
            ub = jnp.concatenate(
                [jnp.concatenate([u[(S5_STEP * blk + i) * BATCH:(S5_STEP * blk + i + 1) * BATCH, :]
                                  for blk in range(n_blk)], axis=0) for i in range(S5_STEP)], axis=1)
            cin = _dot(ub, wk_ref[j])
            if independent:
                independent.pop(0)()
            a_re, a_im = are_ref[j], aim_ref[j]
            state = []
            for s in range(N_SUB):
                rs = slice(SUBLANES * s, SUBLANES * (s + 1))
                state += [st_scr[j, rs, 0:half], st_scr[j, rs, half:2 * half]]
            hp_rows = []
            for blk in range(n_blk):
                for s in range(N_SUB):
                    h_re, h_im = state[2 * s], state[2 * s + 1]
                    hp_rows.append(jnp.concatenate([h_re, h_im], axis=1))
                    b = cin[blk * BATCH + s * SUBLANES:blk * BATCH + (s + 1) * SUBLANES, :]
                    state[2 * s] = a_re * h_re - a_im * h_im + b[:, :half]
                    state[2 * s + 1] = a_re * h_im + a_im * h_re + b[:, half:]
            for s in range(N_SUB):
                rs = slice(SUBLANES * s, SUBLANES * (s + 1))
                st_scr[j, rs, 0:half] = state[2 * s]
                st_scr[j, rs, half:2 * half] = state[2 * s + 1]
            hp = jnp.concatenate(hp_rows, axis=0).astype(BF16)
            yb = _dot(hp, wo_ref[j]) + _dot(ub, tz_ref[j])
            ys.append(jnp.concatenate(
                [yb[blk * BATCH:(blk + 1) * BATCH, i * LANES:(i + 1) * LANES]
                 for blk in range(n_blk) for i in range(S5_STEP)], axis=0))
        while independent:
            independent.pop(0)()
        y = jnp.concatenate(ys, axis=1) + dsk_ref[...] * aval
        y = jax.nn.gelu(y)
        y = y * jax.nn.sigmoid(_dot(y.astype(BF16), wglu_ref[...]) + bglu_ref[...])
        ya_scr[r, :] = (y * jax.nn.silu(gate[0])).astype(BF16)
    _for_chunks(loop_a)

    second = (i % 2) == 1
    t0 = pl.multiple_of((i % 2) * CHUNK, CHUNK)
    for hd in range(SGU_HEADS):
        w_rows = sw_ref[hd, pl.ds(t0, CHUNK), :]
        w_lo, w_hi = w_rows[:, 0:CHUNK], w_rows[:, CHUNK:2 * CHUNK]
        w_cur = jnp.where(second, w_hi, w_lo).astype(BF16)
        w_prev = jnp.where(second, w_lo, 0.0).astype(BF16)
        v_cur = jnp.concatenate(
            [vn_scr[hd, pl.ds((b // SUBLANES) * T_TILE * SUBLANES + b % SUBLANES, CHUNK, stride=SUBLANES), :]
             for b in range(BATCH)], axis=1).astype(BF16)
        zt = _dot(w_cur, v_cur) + _dot(w_prev, vsave_scr[hd]) + sb_ref[hd, pl.ds(t0, CHUNK), :]
        vsave_scr[hd] = v_cur
        for tb in range(CHUNK // SUBLANES):
            z_scr[hd, tb * BATCH * SUBLANES:(tb + 1) * BATCH * SUBLANES, :] = jnp.concatenate(
                [zt[tb * SUBLANES:(tb + 1) * SUBLANES, b * SGU_HEAD_DIM:(b + 1) * SGU_HEAD_DIM]
                 for b in range(BATCH)], axis=0)

    def loop_b(c):
        r = _rows(c)
        hc = h_scr[r, :]

        t_idx = lax.broadcasted_iota(jnp.int32, (ROW_CHUNK, 1), 0) // BATCH
        pos = (i * T_TILE + c * T_CHUNK + 1 + t_idx).astype(F32)
        ys = []
        for gi, w in enumerate(POOL_WINDOWS):
            cols = slice(gi * POOL_GROUP, (gi + 1) * POOL_GROUP)
            cur = ext_scr[_rows(c, base=HALO_ROWS), cols]
            acc = cur
            for k in range(1, w):
                acc = acc + ext_scr[_rows(c, base=HALO_ROWS - k * BATCH), cols]
            p = acc / jnp.minimum(pos, float(w)) - cur
            ys.append(_dot(p.astype(BF16), pw_ref[gi]))
        y = jnp.concatenate(ys, axis=1) * psc_ref[...]
        gate = _dot(hc, win_ref[:, OFF_B_GATE:OFF_B_GATE + D_BRANCH])
        yb = (y * jax.nn.silu(gate)).astype(BF16)

        z = jnp.concatenate(
            [jnp.concatenate(
                [z_scr[hd, pl.ds(_z_row(c, t, bh), SUBLANES, stride=SUBLANES), :]
                 for t in range(T_CHUNK) for bh in range(N_SUB)], axis=0)
             for hd in range(SGU_HEADS)], axis=1)
        u = _dot(hc, win_ref[:, OFF_C_U:OFF_C_U + D_BRANCH])
        gate = _dot(hc, win_ref[:, OFF_C_GATE:OFF_C_GATE + D_BRANCH])
        yc = (u * z * jax.nn.silu(gate)).astype(BF16)

        merged = None
        for k, yk in enumerate((ya_scr[r, :], yb, yc)):
            mg = _dot(hc, win_ref[:, OFF_GATES + k * D_MODEL:OFF_GATES + (k + 1) * D_MODEL])
            term = jax.nn.sigmoid(mg) * _dot(yk, wbr_ref[k])
            merged = term if merged is None else merged + term
        x_res = o_ref[r, :] if first else x_ref[r, :]
        out = x_res + _dot(merged.astype(BF16), wout_ref[...])
        if final:
            ms = jnp.mean(out * out, axis=-1, keepdims=True)
            out = out * lax.rsqrt(ms + RMS_EPS) * fg_ref[...]
            o_ref[:, _rows(c, n=T_CHUNK), :] = jnp.swapaxes(out.reshape(T_CHUNK, BATCH, D_MODEL), 0, 1)
        else:
            o_ref[r, :] = out
    _for_chunks(loop_b)

    ext_scr[0:HALO_ROWS, :] = ext_scr[ROWS:ROWS + HALO_ROWS, :]


def _layer_spec(shape, layer):
    nd = len(shape)
    return pl.BlockSpec((None,) + tuple(shape[1:]), lambda i, _l=layer, _nd=nd: (_l,) + (0,) * (_nd - 1),
                        pipeline_mode=pl.Buffered(1))


def _layer_call(x, params, layer, first, final):
    batch_major = pl.BlockSpec((BATCH, T_TILE, D_MODEL), lambda i: (0, i, 0))
    time_major = pl.BlockSpec((ROWS, D_MODEL), lambda i: (i, 0))
    in_specs = [batch_major if first else time_major] + [
        _layer_spec(p.shape, min(layer, p.shape[0] - 1)) for p in params]
    scratch = [
        pltpu.VMEM((ROWS, D_MODEL), BF16),
        pltpu.VMEM((HALO_ROWS + ROWS, D_BRANCH), F32),
        pltpu.VMEM((S5_BLOCKS, BATCH, 2 * S5_BLOCK_STATE), F32),
        pltpu.VMEM((SGU_HEADS, ROWS, SGU_HEAD_DIM), F32),
        pltpu.VMEM((SGU_HEADS, CHUNK, BATCH * SGU_HEAD_DIM), BF16),
        pltpu.VMEM((SGU_HEADS, ROWS, SGU_HEAD_DIM), F32),
        pltpu.VMEM((ROWS, D_BRANCH), BF16),
    ]
    assert not (first and final), "a single-layer trunk would need a separate time-major input copy"
    return pl.pallas_call(
        functools.partial(_layer_kernel, first=first, final=final),
        grid=(SEQ // T_TILE,),
        in_specs=in_specs,
        out_specs=batch_major if final else time_major,
        out_shape=jax.ShapeDtypeStruct((BATCH, SEQ, D_MODEL) if final else (SEQ * BATCH, D_MODEL), F32),
        scratch_shapes=scratch,
        compiler_params=pltpu.CompilerParams(
            dimension_semantics=("arbitrary",), vmem_limit_bytes=VMEM_LIMIT_BYTES),
        name="trunk_layer_final" if final else "trunk_layer",
    )(x, *params)


def _s5_params(lam_re, lam_im, log_dt, b_re, b_im, c_re, c_im):
    hi = lax.Precision.HIGHEST
    dt = jnp.exp(log_dt)[:, None]
    mag = jnp.exp(lam_re * dt)
    ab_re = mag * jnp.cos(lam_im * dt)
    ab_im = mag * jnp.sin(lam_im * dt)
    den = lam_re * lam_re + lam_im * lam_im
    nr = ab_re - 1.0
    ni = ab_im
    k_re = (nr * lam_re + ni * lam_im) / den
    k_im = (ni * lam_re - nr * lam_im) / den
    bb_re = k_re[..., None] * b_re - k_im[..., None] * b_im
    bb_im = k_re[..., None] * b_im + k_im[..., None] * b_re
    gpb = LANES // S5_GROUP

    pw = [(jnp.ones_like(ab_re), jnp.zeros_like(ab_im))]
    for _ in range(S5_STEP):
        p_re, p_im = pw[-1]
        pw.append((p_re * ab_re - p_im * ab_im, p_re * ab_im + p_im * ab_re))

    def blocks(a):
        a = a.reshape((S5_BLOCKS, gpb) + a.shape[1:])
        eye = jnp.eye(gpb, dtype=a.dtype)
        out = jnp.einsum('jgrc,gh->jgrhc', a, eye)
        return out.reshape(S5_BLOCKS, gpb * a.shape[2], gpb * a.shape[3])

    def rep(a):
        return jnp.broadcast_to(a.reshape(S5_BLOCKS, 1, S5_BLOCK_STATE), (S5_BLOCKS, SUBLANES, S5_BLOCK_STATE))

    wk_rows, wo_cols, tz_rows = [], [], []
    for i in range(S5_STEP):
        p_re, p_im = pw[S5_STEP - 1 - i]
        ab_b_re = (p_re[..., None] * bb_re - p_im[..., None] * bb_im).transpose(0, 2, 1)
        ab_b_im = (p_re[..., None] * bb_im + p_im[..., None] * bb_re).transpose(0, 2, 1)
        wk_rows.append(jnp.concatenate([blocks(ab_b_re), blocks(ab_b_im)], axis=2))
        p_re, p_im = pw[i + 1]
        ca_re = (c_re * p_re[:, None, :] - c_im * p_im[:, None, :]).transpose(0, 2, 1)
        ca_im = (c_re * p_im[:, None, :] + c_im * p_re[:, None, :]).transpose(0, 2, 1)
        wo_cols.append(jnp.concatenate([blocks(ca_re), blocks(-ca_im)], axis=1))
    for m in range(S5_STEP):
        row = []
        for i in range(S5_STEP):
            if m > i:
                row.append(jnp.zeros((S5_BLOCKS, LANES, LANES), jnp.float32))
                continue
            p_re, p_im = pw[i - m]
            ca_re = c_re * p_re[:, None, :] - c_im * p_im[:, None, :]
            ca_im = c_re * p_im[:, None, :] + c_im * p_re[:, None, :]
            t = (jnp.einsum('gop,gpc->gco', ca_re, bb_re, precision=hi)
                 - jnp.einsum('gop,gpc->gco', ca_im, bb_im, precision=hi))
            row.append(blocks(t))
        tz_rows.append(jnp.concatenate(row, axis=2))
    wk = jnp.concatenate(wk_rows, axis=1)
    wo = jnp.concatenate(wo_cols, axis=2)
    tz = jnp.concatenate(tz_rows, axis=1)
    a_re, a_im = pw[S5_STEP]
    return wk.astype(BF16), rep(a_re), rep(a_im), wo.astype(BF16), tz.astype(BF16)


def kernel(x, norm_g, w_in, s5_lam_re, s5_lam_im, s5_log_dt, s5_b_re, s5_b_im, s5_c_re, s5_c_im, s5_d, s5_w_glu, s5_b_glu, pool_w, pool_scale, sgu_ln_g, sgu_ln_b, sgu_w, sgu_b, w_branch, w_out, final_norm_g):
    bsz, seq, d = x.shape
    assert (bsz, seq, d) == (BATCH, SEQ, D_MODEL)
    wk, a_re, a_im, wo, tz = jax.vmap(_s5_params)(s5_lam_re, s5_lam_im, s5_log_dt, s5_b_re, s5_b_im,
                                                  s5_c_re, s5_c_im)
    params = (
        norm_g.reshape(DEPTH, 1, d), w_in.astype(BF16), wk, a_re, a_im, wo, tz,
        s5_d.reshape(DEPTH, 1, D_BRANCH), s5_w_glu.astype(BF16), s5_b_glu.reshape(DEPTH, 1, D_BRANCH),
        pool_w.astype(BF16), pool_scale.reshape(DEPTH, 1, D_BRANCH),
        sgu_ln_g.reshape(DEPTH, 1, D_BRANCH), sgu_ln_b.reshape(DEPTH, 1, D_BRANCH),
        sgu_w, sgu_b.reshape(DEPTH, SGU_HEADS, SGU_BLOCK, 1),
        w_branch.astype(BF16), w_out.astype(BF16), final_norm_g.reshape(1, 1, d),
    )
    for l in range(DEPTH):
        x = _layer_call(x, params, l, first=(l == 0), final=(l == DEPTH - 1))
    return x
```

```python
import functools

import jax
import jax.numpy as jnp
from jax import lax
from jax.experimental import pallas as pl
from jax.experimental.pallas import tpu as pltpu

D_MODEL = 1024
BATCH = 16
SEQ = 2048
DEPTH = 2
CHUNK = 64
D_BRANCH = 512
N_BRANCH = 3
S5_GROUP = 16
S5_GROUPS = D_BRANCH // S5_GROUP
S5_STATE = 64
POOL_WINDOWS = (2, 4, 8, 16)
POOL_GROUP = D_BRANCH // len(POOL_WINDOWS)
SGU_BLOCK = 128
SGU_HEADS = 4
SGU_HEAD_DIM = D_BRANCH // SGU_HEADS
RMS_EPS = 1e-6
LN_EPS = 1e-5

LANES = 128
SUBLANES = 8
S5_BLOCKS = D_BRANCH // LANES
S5_BLOCK_STATE = (LANES // S5_GROUP) * S5_STATE
T_TILE = CHUNK
ROWS = T_TILE * BATCH
ROW_CHUNK = 512
N_CHUNKS = ROWS // ROW_CHUNK
T_CHUNK = ROW_CHUNK // BATCH
HALO_ROWS = max(POOL_WINDOWS) * BATCH
N_SUB = BATCH // SUBLANES
S5_STEP = 2
VMEM_LIMIT_BYTES = 62 * 1024 * 1024

OFF_A_VAL, OFF_A_GATE, OFF_B_VAL, OFF_B_GATE, OFF_C_U, OFF_C_V, OFF_C_GATE, OFF_GATES = (
    0, 512, 1024, 1536, 2048, 2560, 3072, 3584)

F32 = jnp.float32
BF16 = jnp.bfloat16


def _rows(c, n=ROW_CHUNK, base=0):
    if isinstance(c, int):
        return pl.ds(base + c * n, n)
    return pl.ds(pl.multiple_of(base + c * n, SUBLANES), n)


def _z_row(c, t, bh):
    tb, tl = divmod(t, SUBLANES)
    static = (tb * BATCH + bh * SUBLANES) * SUBLANES + tl
    return c * (T_CHUNK * BATCH) + static


def _dot(a, b):
    return jnp.dot(a, b, preferred_element_type=F32)


def _for_chunks(body):
    for c in range(N_CHUNKS):
        body(c)


def _layer_kernel(x_ref, g_ref, win_ref, wk_ref, are_ref, aim_ref, wo_ref, tz_ref, dsk_ref, wglu_ref, bglu_ref,
                  pw_ref, psc_ref, lng_ref, lnb_ref, sw_ref, sb_ref, wbr_ref, wout_ref, fg_ref,
                  o_ref,
                  h_scr, ext_scr, st_scr, vn_scr, vsave_scr, z_scr, ya_scr,
                  *, first, final):
    i = pl.program_id(0)
    half = S5_BLOCK_STATE

    @pl.when(i == 0)
    def _():
        st_scr[...] = jnp.zeros_like(st_scr)
        ext_scr[0:HALO_ROWS, :] = jnp.zeros((HALO_ROWS, D_BRANCH), F32)
        vsave_scr[...] = jnp.zeros_like(vsave_scr)

    def loop_a(c):
        r = _rows(c)
        if first:
            xb = x_ref[:, _rows(c, n=T_CHUNK), :]
            xc = jnp.swapaxes(xb, 0, 1).reshape(ROW_CHUNK, D_MODEL)
            o_ref[r, :] = xc
        else:
            xc = x_ref[r, :]
        ms = jnp.mean(xc * xc, axis=-1, keepdims=True)
        hc = (xc * lax.rsqrt(ms + RMS_EPS) * g_ref[...]).astype(BF16)
        h_scr[r, :] = hc

        def project_pool_value():
            ext_scr[_rows(c, base=HALO_ROWS), :] = _dot(hc, win_ref[:, OFF_B_VAL:OFF_B_VAL + D_BRANCH])

        def project_sgu_value():
            v = _dot(hc, win_ref[:, OFF_C_V:OFF_C_V + D_BRANCH])
            mu = jnp.mean(v, axis=-1, keepdims=True)
            vc = v - mu
            var = jnp.mean(vc * vc, axis=-1, keepdims=True)
            vn = vc * lax.rsqrt(var + LN_EPS) * lng_ref[...] + lnb_ref[...]
            for hd in range(SGU_HEADS):
                for bh in range(N_SUB):
                    slab = jnp.concatenate(
                        [vn[t * BATCH + bh * SUBLANES:t * BATCH + (bh + 1) * SUBLANES,
                            hd * SGU_HEAD_DIM:(hd + 1) * SGU_HEAD_DIM] for t in range(T_CHUNK)], axis=0)
                    vn_scr[hd, _rows(c, n=T_CHUNK * SUBLANES, base=bh * T_TILE * SUBLANES), :] = slab

        gate = []

        def project_s5_gate():
            gate.append(_dot(hc, win_ref[:, OFF_A_GATE:OFF_A_GATE + D_BRANCH]))

        independent = [project_pool_value, project_sgu_value, project_s5_gate]

        aval = _dot(hc, win_ref[:, OFF_A_VAL:OFF_A_VAL + D_BRANCH])
        n_blk = T_CHUNK // S5_STEP
        ys = []
        for j in range(S5_BLOCKS):
            u = aval[:, j * LANES:(j + 1) * LANES].astype(BF16)
            ub = jnp.concatenate(
                [jnp.concatenate([u[(S5_STEP * blk + i) * BATCH:(S5_STEP * blk + i + 1) * BATCH, :]
                                  for blk in range(n_blk)], axis=0) for i in range(S5_STEP)], axis=1)
            cin = _dot(ub, wk_ref[j])
            if independent:
                independent.pop(0)()
            a_re, a_im = are_ref[j], aim_ref[j]
            state = []
            for s in range(N_SUB):
                rs = slice(SUBLANES * s, SUBLANES * (s + 1))
                state += [st_scr[j, rs, 0:half], st_scr[j, rs, half:2 * half]]
            hp_rows = []
            for blk in range(n_blk):
                for s in range(N_SUB):
                    h_re, h_im = state[2 * s], state[2 * s + 1]
                    hp_rows.append(jnp.concatenate([h_re, h_im], axis=1))
                    b = cin[blk * BATCH + s * SUBLANES:blk * BATCH + (s + 1) * SUBLANES, :]
                    state[2 * s] = a_re * h_re - a_im * h_im + b[:, :half]
                    state[2 * s + 1] = a_re * h_im + a_im * h_re + b[:, half:]
            for s in range(N_SUB):
                rs = slice(SUBLANES * s, SUBLANES * (s + 1))
                st_scr[j, rs, 0:half] = state[2 * s]
                st_scr[j, rs, half:2 * half] = state[2 * s + 1]
            hp = jnp.concatenate(hp_rows, axis=0).astype(BF16)
            yb = _dot(hp, wo_ref[j]) + _dot(ub, tz_ref[j])
            ys.append(jnp.concatenate(
                [yb[blk * BATCH:(blk + 1) * BATCH, i * LANES:(i + 1) * LANES]
                 for blk in range(n_blk) for i in range(S5_STEP)], axis=0))
        while independent:
            independent.pop(0)()
        y = jnp.concatenate(ys, axis=1) + dsk_ref[...] * aval
        y = jax.nn.gelu(y)
        y = y * jax.nn.sigmoid(_dot(y.astype(BF16), wglu_ref[...]) + bglu_ref[...])
        ya_scr[r, :] = (y * jax.nn.silu(gate[0])).astype(BF16)
    _for_chunks(loop_a)

    second = (i % 2) == 1
    t0 = pl.multiple_of((i % 2) * CHUNK, CHUNK)
    for hd in range(SGU_HEADS):
        w_rows = sw_ref[hd, pl.ds(t0, CHUNK), :]
        w_lo, w_hi = w_rows[:, 0:CHUNK], w_rows[:, CHUNK:2 * CHUNK]
        w_cur = jnp.where(second, w_hi, w_lo).astype(BF16)
        w_prev = jnp.where(second, w_lo, 0.0).astype(BF16)
        v_cur = jnp.concatenate(
            [vn_scr[hd, pl.ds((b // SUBLANES) * T_TILE * SUBLANES + b % SUBLANES, CHUNK, stride=SUBLANES), :]
             for b in range(BATCH)], axis=1).astype(BF16)
        zt = _dot(w_cur, v_cur) + _dot(w_prev, vsave_scr[hd]) + sb_ref[hd, pl.ds(t0, CHUNK), :]
        vsave_scr[hd] = v_cur
        for tb in range(CHUNK // SUBLANES):
            z_scr[hd, tb * BATCH * SUBLANES:(tb + 1) * BATCH * SUBLANES, :] = jnp.concatenate(
                [zt[tb * SUBLANES:(tb + 1) * SUBLANES, b * SGU_HEAD_DIM:(b + 1) * SGU_HEAD_DIM]
                 for b in range(BATCH)], axis=0)

    def loop_b(c):
        r = _rows(c)
        hc = h_scr[r, :]

        t_idx = lax.broadcasted_iota(jnp.int32, (ROW_CHUNK, 1), 0) // BATCH
        pos = (i * T_TILE + c * T_CHUNK + 1 + t_idx).astype(F32)
        ys = []
        for gi, w in enumerate(POOL_WINDOWS):
            cols = slice(gi * POOL_GROUP, (gi + 1) * POOL_GROUP)
            cur = ext_scr[_rows(c, base=HALO_ROWS), cols]
            acc = cur
            for k in range(1, w):
                acc = acc + ext_scr[_rows(c, base=HALO_ROWS - k * BATCH), cols]
            p = acc / jnp.minimum(pos, float(w)) - cur
            ys.append(_dot(p.astype(BF16), pw_ref[gi]))
        y = jnp.concatenate(ys, axis=1) * psc_ref[...]
        gate = _dot(hc, win_ref[:, OFF_B_GATE:OFF_B_GATE + D_BRANCH])
        yb = (y * jax.nn.silu(gate)).astype(BF16)

        z = jnp.concatenate(
            [jnp.concatenate(
                [z_scr[hd, pl.ds(_z_row(c, t, bh), SUBLANES, stride=SUBLANES), :]
                 for t in range(T_CHUNK) for bh in range(N_SUB)], axis=0)
             for hd in range(SGU_HEADS)], axis=1)
        u = _dot(hc, win_ref[:, OFF_C_U:OFF_C_U + D_BRANCH])
        gate = _dot(hc, win_ref[:, OFF_C_GATE:OFF_C_GATE + D_BRANCH])
        yc = (u * z * jax.nn.silu(gate)).astype(BF16)

        merged = None
        for k, yk in enumerate((ya_scr[r, :], yb, yc)):
            mg = _dot(hc, win_ref[:, OFF_GATES + k * D_MODEL:OFF_GATES + (k + 1) * D_MODEL])
            term = jax.nn.sigmoid(mg) * _dot(yk, wbr_ref[k])
            merged = term if merged is None else merged + term
        x_res = o_ref[r, :] if first else x_ref[r, :]
        out = x_res + _dot(merged.astype(BF16), wout_ref[...])
        if final:
            ms = jnp.mean(out * out, axis=-1, keepdims=True)
            out = out * lax.rsqrt(ms + RMS_EPS) * fg_ref[...]
            o_ref[:, _rows(c, n=T_CHUNK), :] = jnp.swapaxes(out.reshape(T_CHUNK, BATCH, D_MODEL), 0, 1)
        else:
            o_ref[r, :] = out
    _for_chunks(loop_b)

    ext_scr[0:HALO_ROWS, :] = ext_scr[ROWS:ROWS + HALO_ROWS, :]


def _layer_spec(shape, layer):
    nd = len(shape)
    return pl.BlockSpec((None,) + tuple(shape[1:]), lambda i, _l=layer, _nd=nd: (_l,) + (0,) * (_nd - 1),
                        pipeline_mode=pl.Buffered(1))


def _layer_call(x, params, layer, first, final):
    batch_major = pl.BlockSpec((BATCH, T_TILE, D_MODEL), lambda i: (0, i, 0))
    time_major = pl.BlockSpec((ROWS, D_MODEL), lambda i: (i, 0))
    in_specs = [batch_major if first else time_major] + [
        _layer_spec(p.shape, min(layer, p.shape[0] - 1)) for p in params]
    scratch = [
        pltpu.VMEM((ROWS, D_MODEL), BF16),
        pltpu.VMEM((HALO_ROWS + ROWS, D_BRANCH), F32),
        pltpu.VMEM((S5_BLOCKS, BATCH, 2 * S5_BLOCK_STATE), F32),
        pltpu.VMEM((SGU_HEADS, ROWS, SGU_HEAD_DIM), F32),
        pltpu.VMEM((SGU_HEADS, CHUNK, BATCH * SGU_HEAD_DIM), BF16),
        pltpu.VMEM((SGU_HEADS, ROWS, SGU_HEAD_DIM), F32),
        pltpu.VMEM((ROWS, D_BRANCH), BF16),
    ]
    assert not (first and final), "a single-layer trunk would need a separate time-major input copy"
    return pl.pallas_call(
        functools.partial(_layer_kernel, first=first, final=final),
        grid=(SEQ // T_TILE,),
        in_specs=in_specs,
        out_specs=batch_major if final else time_major,
        out_shape=jax.ShapeDtypeStruct((BATCH, SEQ, D_MODEL) if final else (SEQ * BATCH, D_MODEL), F32),
        scratch_shapes=scratch,
        compiler_params=pltpu.CompilerParams(
            dimension_semantics=("arbitrary",), vmem_limit_bytes=VMEM_LIMIT_BYTES),
        name="trunk_layer_final" if final else "trunk_layer",
    )(x, *params)


def _s5_params(lam_re, lam_im, log_dt, b_re, b_im, c_re, c_im):
    hi = lax.Precision.HIGHEST
    dt = jnp.exp(log_dt)[:, None]
    mag = jnp.exp(lam_re * dt)
    ab_re = mag * jnp.cos(lam_im * dt)
    ab_im = mag * jnp.sin(lam_im * dt)
    den = lam_re * lam_re + lam_im * lam_im
    nr = ab_re - 1.0
    ni = ab_im
    k_re = (nr * lam_re + ni * lam_im) / den
    k_im = (ni * lam_re - nr * lam_im) / den
    bb_re = k_re[..., None] * b_re - k_im[..., None] * b_im
    bb_im = k_re[..., None] * b_im + k_im[..., None] * b_re
    gpb = LANES // S5_GROUP

    pw = [(jnp.ones_like(ab_re), jnp.zeros_like(ab_im))]
    for _ in range(S5_STEP):
        p_re, p_im = pw[-1]
        pw.append((p_re * ab_re - p_im * ab_im, p_re * ab_im + p_im * ab_re))
    p_re = jnp.stack([p[0] for p in pw])
    p_im = jnp.stack([p[1] for p in pw])

    anb_re = p_re[..., None] * bb_re - p_im[..., None] * bb_im
    anb_im = p_re[..., None] * bb_im + p_im[..., None] * bb_re
    can_re = c_re * p_re[:, :, None, :] - c_im * p_im[:, :, None, :]
    can_im = c_re * p_im[:, :, None, :] + c_im * p_re[:, :, None, :]

    def place(m, spec, rows, cols):
        m = m.reshape((S5_BLOCKS, gpb) + m.shape[1:])
        return jnp.einsum(spec, m, jnp.eye(gpb, dtype=m.dtype)).reshape(S5_BLOCKS, rows, cols)

    n_in = jnp.arange(S5_STEP - 1, -1, -1)
    wk_g = jnp.stack([anb_re[n_in], anb_im[n_in]], axis=0)
    wk = place(wk_g.transpose(2, 1, 4, 0, 3), 'jgicqp,gh->jigcqhp',
               S5_STEP * LANES, 2 * S5_BLOCK_STATE)
    n_out = jnp.arange(1, S5_STEP + 1)
    wo_g = jnp.stack([can_re[n_out], -can_im[n_out]], axis=0)
    wo = place(wo_g.transpose(2, 0, 4, 1, 3), 'jgqpic,gh->jqgpihc',
               2 * S5_BLOCK_STATE, S5_STEP * LANES)
    t_n = (jnp.einsum('ngop,gpc->ngco', can_re, bb_re, precision=hi)
           - jnp.einsum('ngop,gpc->ngco', can_im, bb_im, precision=hi))
    lag = jnp.arange(S5_STEP)[None, :] - jnp.arange(S5_STEP)[:, None]
    tz_g = jnp.where((lag >= 0)[:, :, None, None, None], t_n[jnp.maximum(lag, 0)], 0.0)
    tz = place(tz_g.transpose(2, 0, 3, 1, 4), 'jgmcio,gh->jmgciho', S5_STEP * LANES, S5_STEP * LANES)

    def rep(a):
        return jnp.broadcast_to(a.reshape(S5_BLOCKS, 1, S5_BLOCK_STATE), (S5_BLOCKS, SUBLANES, S5_BLOCK_STATE))
    return wk.astype(BF16), rep(p_re[S5_STEP]), rep(p_im[S5_STEP]), wo.astype(BF16), tz.astype(BF16)


def kernel(x, norm_g, w_in, s5_lam_re, s5_lam_im, s5_log_dt, s5_b_re, s5_b_im, s5_c_re, s5_c_im, s5_d, s5_w_glu, s5_b_glu, pool_w, pool_scale, sgu_ln_g, sgu_ln_b, sgu_w, sgu_b, w_branch, w_out, final_norm_g):
    bsz, seq, d = x.shape
    assert (bsz, seq, d) == (BATCH, SEQ, D_MODEL)
    wk, a_re, a_im, wo, tz = jax.vmap(_s5_params)(s5_lam_re, s5_lam_im, s5_log_dt, s5_b_re, s5_b_im,
                                                  s5_c_re, s5_c_im)
    params = (
        norm_g.reshape(DEPTH, 1, d), w_in.astype(BF16), wk, a_re, a_im, wo, tz,
        s5_d.reshape(DEPTH, 1, D_BRANCH), s5_w_glu.astype(BF16), s5_b_glu.reshape(DEPTH, 1, D_BRANCH),
        pool_w.astype(BF16), pool_scale.reshape(DEPTH, 1, D_BRANCH),
        sgu_ln_g.reshape(DEPTH, 1, D_BRANCH), sgu_ln_b.reshape(DEPTH, 1, D_BRANCH),
        sgu_w, sgu_b.reshape(DEPTH, SGU_HEADS, SGU_BLOCK, 1),
        w_branch.astype(BF16), w_out.astype(BF16), final_norm_g.reshape(1, 1, d),
    )
    for l in range(DEPTH):
        x = _layer_call(x, params, l, first=(l == 0), final=(l == DEPTH - 1))
    return x
```

```python
import functools

import jax
import jax.numpy as jnp
import numpy as np
from jax import lax
from jax.experimental import pallas as pl
from jax.experimental.pallas import tpu as pltpu

D_MODEL = 1024
BATCH = 16
SEQ = 2048
DEPTH = 2
CHUNK = 64
D_BRANCH = 512
N_BRANCH = 3
S5_GROUP = 16
S5_GROUPS = D_BRANCH // S5_GROUP
S5_STATE = 64
POOL_WINDOWS = (2, 4, 8, 16)
POOL_GROUP = D_BRANCH // len(POOL_WINDOWS)
SGU_BLOCK = 128
SGU_HEADS = 4
SGU_HEAD_DIM = D_BRANCH // SGU_HEADS
RMS_EPS = 1e-6
LN_EPS = 1e-5

LANES = 128
SUBLANES = 8
S5_BLOCKS = D_BRANCH // LANES
S5_BLOCK_STATE = (LANES // S5_GROUP) * S5_STATE
T_TILE = CHUNK
ROWS = T_TILE * BATCH
ROW_CHUNK = 512
N_CHUNKS = ROWS // ROW_CHUNK
T_CHUNK = ROW_CHUNK // BATCH
HALO_ROWS = max(POOL_WINDOWS) * BATCH
N_SUB = BATCH // SUBLANES
S5_STEP = 2
VMEM_LIMIT_BYTES = 62 * 1024 * 1024

OFF_A_VAL, OFF_A_GATE, OFF_B_VAL, OFF_B_GATE, OFF_C_U, OFF_C_V, OFF_C_GATE, OFF_GATES = (
    0, 512, 1024, 1536, 2048, 2560, 3072, 3584)

F32 = jnp.float32
BF16 = jnp.bfloat16


def _rows(c, n=ROW_CHUNK, base=0):
    if isinstance(c, int):
        return pl.ds(base + c * n, n)
    return pl.ds(pl.multiple_of(base + c * n, SUBLANES), n)


def _z_row(c, t, bh):
    tb, tl = divmod(t, SUBLANES)
    static = (tb * BATCH + bh * SUBLANES) * SUBLANES + tl
    return c * (T_CHUNK * BATCH) + static


def _dot(a, b):
    return jnp.dot(a, b, preferred_element_type=F32)


def _for_chunks(body):
    for c in range(N_CHUNKS):
        body(c)


def _layer_kernel(x_ref, g_ref, win_ref, wk_ref, are_ref, aim_ref, wo_ref, tz_ref, dsk_ref, wglu_ref, bglu_ref,
                  pw_ref, psc_ref, lng_ref, lnb_ref, sw_ref, sb_ref, wbr_ref, wout_ref, fg_ref,
                  o_ref,
                  h_scr, ext_scr, st_scr, vn_scr, vsave_scr, z_scr, ya_scr,
                  *, final):
    i = pl.program_id(0)
    half = S5_BLOCK_STATE

    @pl.when(i == 0)
    def _():
        st_scr[...] = jnp.zeros_like(st_scr)
        ext_scr[0:HALO_ROWS, :] = jnp.zeros((HALO_ROWS, D_BRANCH), F32)
        vsave_scr[...] = jnp.zeros_like(vsave_scr)

    def loop_a(c):
        r = _rows(c)
        xc = x_ref[r, :]
        ms = jnp.mean(xc * xc, axis=-1, keepdims=True)
        hc = (xc * lax.rsqrt(ms + RMS_EPS) * g_ref[...]).astype(BF16)
        h_scr[r, :] = hc

        def project_pool_value():
            ext_scr[_rows(c, base=HALO_ROWS), :] = _dot(hc, win_ref[:, OFF_B_VAL:OFF_B_VAL + D_BRANCH])

        def project_sgu_value():
            v = _dot(hc, win_ref[:, OFF_C_V:OFF_C_V + D_BRANCH])
            mu = jnp.mean(v, axis=-1, keepdims=True)
            vc = v - mu
            var = jnp.mean(vc * vc, axis=-1, keepdims=True)
            vn = vc * lax.rsqrt(var + LN_EPS) * lng_ref[...] + lnb_ref[...]
            for hd in range(SGU_HEADS):
                for bh in range(N_SUB):
                    slab = jnp.concatenate(
                        [vn[t * BATCH + bh * SUBLANES:t * BATCH + (bh + 1) * SUBLANES,
                            hd * SGU_HEAD_DIM:(hd + 1) * SGU_HEAD_DIM] for t in range(T_CHUNK)], axis=0)
                    vn_scr[hd, _rows(c, n=T_CHUNK * SUBLANES, base=bh * T_TILE * SUBLANES), :] = slab

        gate = []

        def project_s5_gate():
            gate.append(_dot(hc, win_ref[:, OFF_A_GATE:OFF_A_GATE + D_BRANCH]))

        independent = [project_pool_value, project_sgu_value, project_s5_gate]

        aval = _dot(hc, win_ref[:, OFF_A_VAL:OFF_A_VAL + D_BRANCH])
        n_blk = T_CHUNK // S5_STEP
        ys = []
        for j in range(S5_BLOCKS):
            u = aval[:, j * LANES:(j + 1) * LANES].astype(BF16)
            ub = jnp.concatenate(
                [jnp.concatenate([u[(S5_STEP * blk + i) * BATCH:(S5_STEP * blk + i + 1) * BATCH, :]
                                  for blk in range(n_blk)], axis=0) for i in range(S5_STEP)], axis=1)
            cin = _dot(ub, wk_ref[j])
            if independent:
                independent.pop(0)()
            a_re, a_im = are_ref[j], aim_ref[j]
            state = []
            for s in range(N_SUB):
                rs = slice(SUBLANES * s, SUBLANES * (s + 1))
                state += [st_scr[j, rs, 0:half], st_scr[j, rs, half:2 * half]]
            hp_rows = []
            for blk in range(n_blk):
                for s in range(N_SUB):
                    h_re, h_im = state[2 * s], state[2 * s + 1]
                    hp_rows.append(jnp.concatenate([h_re, h_im], axis=1))
                    b = cin[blk * BATCH + s * SUBLANES:blk * BATCH + (s + 1) * SUBLANES, :]
                    state[2 * s] = a_re * h_re - a_im * h_im + b[:, :half]
                    state[2 * s + 1] = a_re * h_im + a_im * h_re + b[:, half:]
            for s in range(N_SUB):
                rs = slice(SUBLANES * s, SUBLANES * (s + 1))
                st_scr[j, rs, 0:half] = state[2 * s]
                st_scr[j, rs, half:2 * half] = state[2 * s + 1]
            hp = jnp.concatenate(hp_rows, axis=0).astype(BF16)
            yb = _dot(hp, wo_ref[j]) + _dot(ub, tz_ref[j])
            ys.append(jnp.concatenate(
                [yb[blk * BATCH:(blk + 1) * BATCH, i * LANES:(i + 1) * LANES]
                 for blk in range(n_blk) for i in range(S5_STEP)], axis=0))
        while independent:
            independent.pop(0)()
        y = jnp.concatenate(ys, axis=1) + dsk_ref[...] * aval
        y = jax.nn.gelu(y)
        y = y * jax.nn.sigmoid(_dot(y.astype(BF16), wglu_ref[...]) + bglu_ref[...])
        ya_scr[r, :] = (y * jax.nn.silu(gate[0])).astype(BF16)
    _for_chunks(loop_a)

    second = (i % 2) == 1
    t0 = pl.multiple_of((i % 2) * CHUNK, CHUNK)
    for hd in range(SGU_HEADS):
        w_rows = sw_ref[hd, pl.ds(t0, CHUNK), :]
        w_lo, w_hi = w_rows[:, 0:CHUNK], w_rows[:, CHUNK:2 * CHUNK]
        w_cur = jnp.where(second, w_hi, w_lo).astype(BF16)
        w_prev = jnp.where(second, w_lo, 0.0).astype(BF16)
        v_cur = jnp.concatenate(
            [vn_scr[hd, pl.ds((b // SUBLANES) * T_TILE * SUBLANES + b % SUBLANES, CHUNK, stride=SUBLANES), :]
             for b in range(BATCH)], axis=1).astype(BF16)
        zt = _dot(w_cur, v_cur) + _dot(w_prev, vsave_scr[hd]) + sb_ref[hd, pl.ds(t0, CHUNK), :]
        vsave_scr[hd] = v_cur
        for tb in range(CHUNK // SUBLANES):
            z_scr[hd, tb * BATCH * SUBLANES:(tb + 1) * BATCH * SUBLANES, :] = jnp.concatenate(
                [zt[tb * SUBLANES:(tb + 1) * SUBLANES, b * SGU_HEAD_DIM:(b + 1) * SGU_HEAD_DIM]
                 for b in range(BATCH)], axis=0)

    def loop_b(c):
        r = _rows(c)
        hc = h_scr[r, :]

        t_idx = lax.broadcasted_iota(jnp.int32, (ROW_CHUNK, 1), 0) // BATCH
        pos = (i * T_TILE + c * T_CHUNK + 1 + t_idx).astype(F32)
        ys = []
        for gi, w in enumerate(POOL_WINDOWS):
            cols = slice(gi * POOL_GROUP, (gi + 1) * POOL_GROUP)
            cur = ext_scr[_rows(c, base=HALO_ROWS), cols]
            acc = cur
            for k in range(1, w):
                acc = acc + ext_scr[_rows(c, base=HALO_ROWS - k * BATCH), cols]
            p = acc / jnp.minimum(pos, float(w)) - cur
            ys.append(_dot(p.astype(BF16), pw_ref[gi]))
        y = jnp.concatenate(ys, axis=1) * psc_ref[...]
        gate = _dot(hc, win_ref[:, OFF_B_GATE:OFF_B_GATE + D_BRANCH])
        yb = (y * jax.nn.silu(gate)).astype(BF16)

        z = jnp.concatenate(
            [jnp.concatenate(
                [z_scr[hd, pl.ds(_z_row(c, t, bh), SUBLANES, stride=SUBLANES), :]
                 for t in range(T_CHUNK) for bh in range(N_SUB)], axis=0)
             for hd in range(SGU_HEADS)], axis=1)
        u = _dot(hc, win_ref[:, OFF_C_U:OFF_C_U + D_BRANCH])
        gate = _dot(hc, win_ref[:, OFF_C_GATE:OFF_C_GATE + D_BRANCH])
        yc = (u * z * jax.nn.silu(gate)).astype(BF16)

        merged = None
        for k, yk in enumerate((ya_scr[r, :], yb, yc)):
            mg = _dot(hc, win_ref[:, OFF_GATES + k * D_MODEL:OFF_GATES + (k + 1) * D_MODEL])
            term = jax.nn.sigmoid(mg) * _dot(yk, wbr_ref[k])
            merged = term if merged is None else merged + term
        out = x_ref[r, :] + _dot(merged.astype(BF16), wout_ref[...])
        if final:
            ms = jnp.mean(out * out, axis=-1, keepdims=True)
            out = out * lax.rsqrt(ms + RMS_EPS) * fg_ref[...]
            o_ref[:, _rows(c, n=T_CHUNK), :] = jnp.swapaxes(out.reshape(T_CHUNK, BATCH, D_MODEL), 0, 1)
        else:
            o_ref[r, :] = out
    _for_chunks(loop_b)

    ext_scr[0:HALO_ROWS, :] = ext_scr[ROWS:ROWS + HALO_ROWS, :]


def _layer_spec(shape, layer):
    nd = len(shape)
    return pl.BlockSpec((None,) + tuple(shape[1:]), lambda i, _l=layer, _nd=nd: (_l,) + (0,) * (_nd - 1),
                        pipeline_mode=pl.Buffered(1))


def _layer_call(x, params, layer, final):
    batch_major = pl.BlockSpec((BATCH, T_TILE, D_MODEL), lambda i: (0, i, 0))
    time_major = pl.BlockSpec((ROWS, D_MODEL), lambda i: (i, 0))
    in_specs = [time_major] + [_layer_spec(p.shape, min(layer, p.shape[0] - 1)) for p in params]
    scratch = [
        pltpu.VMEM((ROWS, D_MODEL), BF16),
        pltpu.VMEM((HALO_ROWS + ROWS, D_BRANCH), F32),
        pltpu.VMEM((S5_BLOCKS, BATCH, 2 * S5_BLOCK_STATE), F32),
        pltpu.VMEM((SGU_HEADS, ROWS, SGU_HEAD_DIM), F32),
        pltpu.VMEM((SGU_HEADS, CHUNK, BATCH * SGU_HEAD_DIM), BF16),
        pltpu.VMEM((SGU_HEADS, ROWS, SGU_HEAD_DIM), F32),
        pltpu.VMEM((ROWS, D_BRANCH), BF16),
    ]
    return pl.pallas_call(
        functools.partial(_layer_kernel, final=final),
        grid=(SEQ // T_TILE,),
        in_specs=in_specs,
        out_specs=batch_major if final else time_major,
        out_shape=jax.ShapeDtypeStruct((BATCH, SEQ, D_MODEL) if final else (SEQ * BATCH, D_MODEL), F32),
        scratch_shapes=scratch,
        compiler_params=pltpu.CompilerParams(
            dimension_semantics=("arbitrary",), vmem_limit_bytes=VMEM_LIMIT_BYTES),
        name="trunk_layer_final" if final else "trunk_layer",
    )(x, *params)


def _expansion(n_compact, n_full, compact_of_full):
    e = np.zeros((n_compact, n_full), np.float32)
    e[compact_of_full, np.arange(n_full)] = 1.0
    return e


def _s5_constants():
    gpb = LANES // S5_GROUP
    q, h, p = np.meshgrid(np.arange(2), np.arange(gpb), np.arange(S5_STATE), indexing='ij')
    e_state = _expansion(2 * S5_STATE, 2 * S5_BLOCK_STATE, (q * S5_STATE + p).ravel())
    i, g, c = np.meshgrid(np.arange(S5_STEP), np.arange(gpb), np.arange(S5_GROUP), indexing='ij')
    e_chan = _expansion(S5_STEP * S5_GROUP, S5_STEP * LANES, (i * S5_GROUP + c).ravel())
    same_state = (g.ravel()[:, None] == h.ravel()[None, :]).astype(np.float32)
    same_chan = (g.ravel()[:, None] == g.ravel()[None, :]).astype(np.float32)
    return e_state, e_chan, same_state, same_chan


def _s5_params(lam_re, lam_im, log_dt, b_re, b_im, c_re, c_im):
    hi = lax.Precision.HIGHEST
    dt = jnp.exp(log_dt)[:, None]
    mag = jnp.exp(lam_re * dt)
    ab_re = mag * jnp.cos(lam_im * dt)
    ab_im = mag * jnp.sin(lam_im * dt)
    den = lam_re * lam_re + lam_im * lam_im
    nr = ab_re - 1.0
    ni = ab_im
    k_re = (nr * lam_re + ni * lam_im) / den
    k_im = (ni * lam_re - nr * lam_im) / den
    bb_re = k_re[..., None] * b_re - k_im[..., None] * b_im
    bb_im = k_re[..., None] * b_im + k_im[..., None] * b_re
    gpb = LANES // S5_GROUP

    pw = [(jnp.ones_like(ab_re), jnp.zeros_like(ab_im))]
    for _ in range(S5_STEP):
        p_re, p_im = pw[-1]
        pw.append((p_re * ab_re - p_im * ab_im, p_re * ab_im + p_im * ab_re))
    p_re = jnp.stack([p[0] for p in pw])
    p_im = jnp.stack([p[1] for p in pw])

    anb_re = p_re[..., None] * bb_re - p_im[..., None] * bb_im
    anb_im = p_re[..., None] * bb_im + p_im[..., None] * bb_re
    can_re = c_re * p_re[:, :, None, :] - c_im * p_im[:, :, None, :]
    can_im = c_re * p_im[:, :, None, :] + c_im * p_re[:, :, None, :]

    e_state, e_chan, same_state, same_chan = _s5_constants()

    def by_block(m):
        return m.reshape(m.shape[:2] + (S5_BLOCKS, gpb) + m.shape[3:])

    n_in = jnp.arange(S5_STEP - 1, -1, -1)
    wk_c = by_block(jnp.stack([anb_re[n_in], anb_im[n_in]]))
    wk_c = wk_c.transpose(2, 1, 3, 5, 0, 4).reshape(S5_BLOCKS, S5_STEP * LANES, 2 * S5_STATE)
    wk = jnp.einsum('jrs,st->jrt', wk_c, e_state, precision=hi) * same_state
    n_out = jnp.arange(1, S5_STEP + 1)
    wo_c = by_block(jnp.stack([can_re[n_out], -can_im[n_out]]))
    wo_c = wo_c.transpose(2, 0, 5, 1, 3, 4).reshape(S5_BLOCKS, 2 * S5_STATE, S5_STEP * LANES)
    wo = jnp.einsum('ts,jsr->jtr', e_state.T, wo_c, precision=hi) * same_state.T
    t_n = (jnp.einsum('ngop,gpc->ngco', can_re, bb_re, precision=hi)
           - jnp.einsum('ngop,gpc->ngco', can_im, bb_im, precision=hi))
    lag = np.arange(S5_STEP)[None, :] - np.arange(S5_STEP)[:, None]
    tz_c = jnp.where((lag >= 0)[:, :, None, None, None], t_n[np.maximum(lag, 0)], 0.0)
    tz_c = by_block(tz_c).transpose(2, 0, 3, 4, 1, 5).reshape(S5_BLOCKS, S5_STEP * LANES, S5_STEP * S5_GROUP)
    tz = jnp.einsum('jrs,st->jrt', tz_c, e_chan, precision=hi) * same_chan

    def rep(a):
        return jnp.broadcast_to(a.reshape(S5_BLOCKS, 1, S5_BLOCK_STATE), (S5_BLOCKS, SUBLANES, S5_BLOCK_STATE))
    return wk.astype(BF16), rep(p_re[S5_STEP]), rep(p_im[S5_STEP]), wo.astype(BF16), tz.astype(BF16)


def kernel(x, norm_g, w_in, s5_lam_re, s5_lam_im, s5_log_dt, s5_b_re, s5_b_im, s5_c_re, s5_c_im, s5_d, s5_w_glu, s5_b_glu, pool_w, pool_scale, sgu_ln_g, sgu_ln_b, sgu_w, sgu_b, w_branch, w_out, final_norm_g):
    bsz, seq, d = x.shape
    assert (bsz, seq, d) == (BATCH, SEQ, D_MODEL)
    wk, a_re, a_im, wo, tz = jax.vmap(_s5_params)(s5_lam_re, s5_lam_im, s5_log_dt, s5_b_re, s5_b_im,
                                                  s5_c_re, s5_c_im)
    params = (
        norm_g.reshape(DEPTH, 1, d), w_in.astype(BF16), wk, a_re, a_im, wo, tz,
        s5_d.reshape(DEPTH, 1, D_BRANCH), s5_w_glu.astype(BF16), s5_b_glu.reshape(DEPTH, 1, D_BRANCH),
        pool_w.astype(BF16), pool_scale.reshape(DEPTH, 1, D_BRANCH),
        sgu_ln_g.reshape(DEPTH, 1, D_BRANCH), sgu_ln_b.reshape(DEPTH, 1, D_BRANCH),
        sgu_w, sgu_b.reshape(DEPTH, SGU_HEADS, SGU_BLOCK, 1),
        w_branch.astype(BF16), w_out.astype(BF16), final_norm_g.reshape(1, 1, d),
    )
    x = x.transpose(1, 0, 2).reshape(seq * bsz, d)
    for l in range(DEPTH):
        x = _layer_call(x, params, l, final=(l == DEPTH - 1))
    return x
```

```python
import functools

import jax
import jax.numpy as jnp
import numpy as np
from jax import lax
from jax.experimental import pallas as pl
from jax.experimental.pallas import tpu as pltpu

D_MODEL = 1024
BATCH = 16
SEQ = 2048
DEPTH = 2
CHUNK = 64
D_BRANCH = 512
N_BRANCH = 3
S5_GROUP = 16
S5_GROUPS = D_BRANCH // S5_GROUP
S5_STATE = 64
POOL_WINDOWS = (2, 4, 8, 16)
POOL_GROUP = D_BRANCH // len(POOL_WINDOWS)
SGU_BLOCK = 128
SGU_HEADS = 4
SGU_HEAD_DIM = D_BRANCH // SGU_HEADS
RMS_EPS = 1e-6
LN_EPS = 1e-5

LANES = 128
SUBLANES = 8
S5_BLOCKS = D_BRANCH // LANES
S5_BLOCK_STATE = (LANES // S5_GROUP) * S5_STATE
T_TILE = CHUNK
ROWS = T_TILE * BATCH
ROW_CHUNK = 512
N_CHUNKS = ROWS // ROW_CHUNK
T_CHUNK = ROW_CHUNK // BATCH
HALO_ROWS = max(POOL_WINDOWS) * BATCH
N_SUB = BATCH // SUBLANES
S5_STEP = 2
VMEM_LIMIT_BYTES = 62 * 1024 * 1024

OFF_A_VAL, OFF_A_GATE, OFF_B_VAL, OFF_B_GATE, OFF_C_U, OFF_C_V, OFF_C_GATE, OFF_GATES = (
    0, 512, 1024, 1536, 2048, 2560, 3072, 3584)

F32 = jnp.float32
BF16 = jnp.bfloat16


def _rows(c, n=ROW_CHUNK, base=0):
    if isinstance(c, int):
        return pl.ds(base + c * n, n)
    return pl.ds(pl.multiple_of(base + c * n, SUBLANES), n)


def _z_row(c, t, bh):
    tb, tl = divmod(t, SUBLANES)
    static = (tb * BATCH + bh * SUBLANES) * SUBLANES + tl
    return c * (T_CHUNK * BATCH) + static


def _dot(a, b):
    return jnp.dot(a, b, preferred_element_type=F32)


def _for_chunks(body):
    for c in range(N_CHUNKS):
        body(c)


def _tile_copies(hbm_ref, buf_ref, sem_ref, tile, slot, to_hbm):
    copies = []
    for b in range(BATCH):
        hbm = hbm_ref.at[b, pl.ds(tile * T_TILE, T_TILE), :]
        vmem = buf_ref.at[slot, :, b, :]
        src, dst = (vmem, hbm) if to_hbm else (hbm, vmem)
        copies.append(pltpu.make_async_copy(src, dst, sem_ref.at[slot, b]))
    return copies


def _layer_kernel(x_ref, g_ref, win_ref, wk_ref, are_ref, aim_ref, wo_ref, tz_ref, dsk_ref, wglu_ref, bglu_ref,
                  pw_ref, psc_ref, lng_ref, lnb_ref, sw_ref, sb_ref, wbr_ref, wout_ref, fg_ref,
                  o_ref,
                  h_scr, ext_scr, st_scr, vn_scr, vsave_scr, z_scr, ya_scr, *io_scr,
                  first, final):
    i = pl.program_id(0)
    n_tiles = pl.num_programs(0)
    slot = i % 2
    half = S5_BLOCK_STATE
    if first:
        xbuf, xsem = io_scr
    if final:
        obuf, osem = io_scr

    @pl.when(i == 0)
    def _():
        st_scr[...] = jnp.zeros_like(st_scr)
        ext_scr[0:HALO_ROWS, :] = jnp.zeros((HALO_ROWS, D_BRANCH), F32)
        vsave_scr[...] = jnp.zeros_like(vsave_scr)

    if first:
        @pl.when(i == 0)
        def _():
            for cp in _tile_copies(x_ref, xbuf, xsem, 0, 0, to_hbm=False):
                cp.start()

        @pl.when(i + 1 < n_tiles)
        def _():
            for cp in _tile_copies(x_ref, xbuf, xsem, i + 1, 1 - slot, to_hbm=False):
                cp.start()

        for cp in _tile_copies(x_ref, xbuf, xsem, i, slot, to_hbm=False):
            cp.wait()

    if final:
        @pl.when(i >= 2)
        def _():
            for cp in _tile_copies(o_ref, obuf, osem, i - 2, slot, to_hbm=True):
                cp.wait()

    def x_rows(c):
        if first:
            return xbuf[slot, pl.ds(c * T_CHUNK, T_CHUNK), :, :].reshape(ROW_CHUNK, D_MODEL)
        return x_ref[_rows(c), :]

    def loop_a(c):
        r = _rows(c)
        xc = x_rows(c)
        ms = jnp.mean(xc * xc, axis=-1, keepdims=True)
        hc = (xc * lax.rsqrt(ms + RMS_EPS) * g_ref[...]).astype(BF16)
        h_scr[r, :] = hc

        def project_pool_value():
            ext_scr[_rows(c, base=HALO_ROWS), :] = _dot(hc, win_ref[:, OFF_B_VAL:OFF_B_VAL + D_BRANCH])

        def project_sgu_value():
            v = _dot(hc, win_ref[:, OFF_C_V:OFF_C_V + D_BRANCH])
            mu = jnp.mean(v, axis=-1, keepdims=True)
            vc = v - mu
            var = jnp.mean(vc * vc, axis=-1, keepdims=True)
            vn = vc * lax.rsqrt(var + LN_EPS) * lng_ref[...] + lnb_ref[...]
            for hd in range(SGU_HEADS):
                for bh in range(N_SUB):
                    slab = jnp.concatenate(
                        [vn[t * BATCH + bh * SUBLANES:t * BATCH + (bh + 1) * SUBLANES,
                            hd * SGU_HEAD_DIM:(hd + 1) * SGU_HEAD_DIM] for t in range(T_CHUNK)], axis=0)
                    vn_scr[hd, _rows(c, n=T_CHUNK * SUBLANES, base=bh * T_TILE * SUBLANES), :] = slab

        gate = []

        def project_s5_gate():
            gate.append(_dot(hc, win_ref[:, OFF_A_GATE:OFF_A_GATE + D_BRANCH]))

        independent = [project_pool_value, project_sgu_value, project_s5_gate]

        aval = _dot(hc, win_ref[:, OFF_A_VAL:OFF_A_VAL + D_BRANCH])
        n_blk = T_CHUNK // S5_STEP
        ys = []
        for j in range(S5_BLOCKS):
            u = aval[:, j * LANES:(j + 1) * LANES].astype(BF16)
            ub = jnp.concatenate(
                [jnp.concatenate([u[(S5_STEP * blk + i) * BATCH:(S5_STEP * blk + i + 1) * BATCH, :]
                                  for blk in range(n_blk)], axis=0) for i in range(S5_STEP)], axis=1)
            cin = _dot(ub, wk_ref[j])
            if independent:
                independent.pop(0)()
            a_re, a_im = are_ref[j], aim_ref[j]
            state = []
            for s in range(N_SUB):
                rs = slice(SUBLANES * s, SUBLANES * (s + 1))
                state += [st_scr[j, rs, 0:half], st_scr[j, rs, half:2 * half]]
            hp_rows = []
            for blk in range(n_blk):
                for s in range(N_SUB):
                    h_re, h_im = state[2 * s], state[2 * s + 1]
                    hp_rows.append(jnp.concatenate([h_re, h_im], axis=1))
                    b = cin[blk * BATCH + s * SUBLANES:blk * BATCH + (s + 1) * SUBLANES, :]
                    state[2 * s] = a_re * h_re - a_im * h_im + b[:, :half]
                    state[2 * s + 1] = a_re * h_im + a_im * h_re + b[:, half:]
            for s in range(N_SUB):
                rs = slice(SUBLANES * s, SUBLANES * (s + 1))
                st_scr[j, rs, 0:half] = state[2 * s]
                st_scr[j, rs, half:2 * half] = state[2 * s + 1]
            hp = jnp.concatenate(hp_rows, axis=0).astype(BF16)
            yb = _dot(hp, wo_ref[j]) + _dot(ub, tz_ref[j])
            ys.append(jnp.concatenate(
                [yb[blk * BATCH:(blk + 1) * BATCH, i * LANES:(i + 1) * LANES]
                 for blk in range(n_blk) for i in range(S5_STEP)], axis=0))
        while independent:
            independent.pop(0)()
        y = jnp.concatenate(ys, axis=1) + dsk_ref[...] * aval
        y = jax.nn.gelu(y)
        y = y * jax.nn.sigmoid(_dot(y.astype(BF16), wglu_ref[...]) + bglu_ref[...])
        ya_scr[r, :] = (y * jax.nn.silu(gate[0])).astype(BF16)
    _for_chunks(loop_a)

    second = (i % 2) == 1
    t0 = pl.multiple_of((i % 2) * CHUNK, CHUNK)
    for hd in range(SGU_HEADS):
        w_rows = sw_ref[hd, pl.ds(t0, CHUNK), :]
        w_lo, w_hi = w_rows[:, 0:CHUNK], w_rows[:, CHUNK:2 * CHUNK]
        w_cur = jnp.where(second, w_hi, w_lo).astype(BF16)
        w_prev = jnp.where(second, w_lo, 0.0).astype(BF16)
        v_cur = jnp.concatenate(
            [vn_scr[hd, pl.ds((b // SUBLANES) * T_TILE * SUBLANES + b % SUBLANES, CHUNK, stride=SUBLANES), :]
             for b in range(BATCH)], axis=1).astype(BF16)
        zt = _dot(w_cur, v_cur) + _dot(w_prev, vsave_scr[hd]) + sb_ref[hd, pl.ds(t0, CHUNK), :]
        vsave_scr[hd] = v_cur
        for tb in range(CHUNK // SUBLANES):
            z_scr[hd, tb * BATCH * SUBLANES:(tb + 1) * BATCH * SUBLANES, :] = jnp.concatenate(
                [zt[tb * SUBLANES:(tb + 1) * SUBLANES, b * SGU_HEAD_DIM:(b + 1) * SGU_HEAD_DIM]
                 for b in range(BATCH)], axis=0)

    def loop_b(c):
        r = _rows(c)
        hc = h_scr[r, :]

        t_idx = lax.broadcasted_iota(jnp.int32, (ROW_CHUNK, 1), 0) // BATCH
        pos = (i * T_TILE + c * T_CHUNK + 1 + t_idx).astype(F32)
        ys = []
        for gi, w in enumerate(POOL_WINDOWS):
            cols = slice(gi * POOL_GROUP, (gi + 1) * POOL_GROUP)
            cur = ext_scr[_rows(c, base=HALO_ROWS), cols]
            acc = cur
            for k in range(1, w):
                acc = acc + ext_scr[_rows(c, base=HALO_ROWS - k * BATCH), cols]
            p = acc / jnp.minimum(pos, float(w)) - cur
            ys.append(_dot(p.astype(BF16), pw_ref[gi]))
        y = jnp.concatenate(ys, axis=1) * psc_ref[...]
        gate = _dot(hc, win_ref[:, OFF_B_GATE:OFF_B_GATE + D_BRANCH])
        yb = (y * jax.nn.silu(gate)).astype(BF16)

        z = jnp.concatenate(
            [jnp.concatenate(
                [z_scr[hd, pl.ds(_z_row(c, t, bh), SUBLANES, stride=SUBLANES), :]
                 for t in range(T_CHUNK) for bh in range(N_SUB)], axis=0)
             for hd in range(SGU_HEADS)], axis=1)
        u = _dot(hc, win_ref[:, OFF_C_U:OFF_C_U + D_BRANCH])
        gate = _dot(hc, win_ref[:, OFF_C_GATE:OFF_C_GATE + D_BRANCH])
        yc = (u * z * jax.nn.silu(gate)).astype(BF16)

        merged = None
        for k, yk in enumerate((ya_scr[r, :], yb, yc)):
            mg = _dot(hc, win_ref[:, OFF_GATES + k * D_MODEL:OFF_GATES + (k + 1) * D_MODEL])
            term = jax.nn.sigmoid(mg) * _dot(yk, wbr_ref[k])
            merged = term if merged is None else merged + term
        out = x_rows(c) + _dot(merged.astype(BF16), wout_ref[...])
        if final:
            ms = jnp.mean(out * out, axis=-1, keepdims=True)
            out = out * lax.rsqrt(ms + RMS_EPS) * fg_ref[...]
            obuf[slot, pl.ds(c * T_CHUNK, T_CHUNK), :, :] = out.reshape(T_CHUNK, BATCH, D_MODEL)
        else:
            o_ref[r, :] = out
    _for_chunks(loop_b)

    ext_scr[0:HALO_ROWS, :] = ext_scr[ROWS:ROWS + HALO_ROWS, :]

    if final:
        for cp in _tile_copies(o_ref, obuf, osem, i, slot, to_hbm=True):
            cp.start()

        @pl.when(i == n_tiles - 1)
        def _():
            for cp in _tile_copies(o_ref, obuf, osem, i - 1, 1 - slot, to_hbm=True):
                cp.wait()
            for cp in _tile_copies(o_ref, obuf, osem, i, slot, to_hbm=True):
                cp.wait()


def _layer_spec(shape, layer):
    nd = len(shape)
    return pl.BlockSpec((None,) + tuple(shape[1:]), lambda i, _l=layer, _nd=nd: (_l,) + (0,) * (_nd - 1),
                        pipeline_mode=pl.Buffered(1))


def _layer_call(x, params, layer, first, final):
    in_hbm = pl.BlockSpec(memory_space=pl.ANY)
    time_major = pl.BlockSpec((ROWS, D_MODEL), lambda i: (i, 0))
    in_specs = [in_hbm if first else time_major] + [
        _layer_spec(p.shape, min(layer, p.shape[0] - 1)) for p in params]
    scratch = [
        pltpu.VMEM((ROWS, D_MODEL), BF16),
        pltpu.VMEM((HALO_ROWS + ROWS, D_BRANCH), F32),
        pltpu.VMEM((S5_BLOCKS, BATCH, 2 * S5_BLOCK_STATE), F32),
        pltpu.VMEM((SGU_HEADS, ROWS, SGU_HEAD_DIM), F32),
        pltpu.VMEM((SGU_HEADS, CHUNK, BATCH * SGU_HEAD_DIM), BF16),
        pltpu.VMEM((SGU_HEADS, ROWS, SGU_HEAD_DIM), F32),
        pltpu.VMEM((ROWS, D_BRANCH), BF16),
    ]
    assert SEQ // T_TILE >= 2 and not (first and final)
    io_buffer = [pltpu.VMEM((2, T_TILE, BATCH, D_MODEL), F32), pltpu.SemaphoreType.DMA((2, BATCH))]
    if first or final:
        scratch += io_buffer
    return pl.pallas_call(
        functools.partial(_layer_kernel, first=first, final=final),
        grid=(SEQ // T_TILE,),
        in_specs=in_specs,
        out_specs=in_hbm if final else time_major,
        out_shape=jax.ShapeDtypeStruct((BATCH, SEQ, D_MODEL) if final else (SEQ * BATCH, D_MODEL), F32),
        scratch_shapes=scratch,
        compiler_params=pltpu.CompilerParams(
            dimension_semantics=("arbitrary",), vmem_limit_bytes=VMEM_LIMIT_BYTES),
        name="trunk_layer_final" if final else "trunk_layer",
    )(x, *params)


def _expansion(n_compact, n_full, compact_of_full):
    e = np.zeros((n_compact, n_full), np.float32)
    e[compact_of_full, np.arange(n_full)] = 1.0
    return e


def _s5_constants():
    gpb = LANES // S5_GROUP
    q, h, p = np.meshgrid(np.arange(2), np.arange(gpb), np.arange(S5_STATE), indexing='ij')
    e_state = _expansion(2 * S5_STATE, 2 * S5_BLOCK_STATE, (q * S5_STATE + p).ravel())
    i, g, c = np.meshgrid(np.arange(S5_STEP), np.arange(gpb), np.arange(S5_GROUP), indexing='ij')
    e_chan = _expansion(S5_STEP * S5_GROUP, S5_STEP * LANES, (i * S5_GROUP + c).ravel())
    same_state = (g.ravel()[:, None] == h.ravel()[None, :]).astype(np.float32)
    same_chan = (g.ravel()[:, None] == g.ravel()[None, :]).astype(np.float32)
    return e_state, e_chan, same_state, same_chan


def _s5_params(lam_re, lam_im, log_dt, b_re, b_im, c_re, c_im):
    hi = lax.Precision.HIGHEST
    dt = jnp.exp(log_dt)[:, None]
    mag = jnp.exp(lam_re * dt)
    ab_re = mag * jnp.cos(lam_im * dt)
    ab_im = mag * jnp.sin(lam_im * dt)
    den = lam_re * lam_re + lam_im * lam_im
    nr = ab_re - 1.0
    ni = ab_im
    k_re = (nr * lam_re + ni * lam_im) / den
    k_im = (ni * lam_re - nr * lam_im) / den
    bb_re = k_re[..., None] * b_re - k_im[..., None] * b_im
    bb_im = k_re[..., None] * b_im + k_im[..., None] * b_re
    gpb = LANES // S5_GROUP

    pw = [(jnp.ones_like(ab_re), jnp.zeros_like(ab_im))]
    for _ in range(S5_STEP):
        p_re, p_im = pw[-1]
        pw.append((p_re * ab_re - p_im * ab_im, p_re * ab_im + p_im * ab_re))
    p_re = jnp.stack([p[0] for p in pw])
    p_im = jnp.stack([p[1] for p in pw])

    anb_re = p_re[..., None] * bb_re - p_im[..., None] * bb_im
    anb_im = p_re[..., None] * bb_im + p_im[..., None] * bb_re
    can_re = c_re * p_re[:, :, None, :] - c_im * p_im[:, :, None, :]
    can_im = c_re * p_im[:, :, None, :] + c_im * p_re[:, :, None, :]

    e_state, e_chan, same_state, same_chan = _s5_constants()

    def by_block(m):
        return m.reshape(m.shape[:2] + (S5_BLOCKS, gpb) + m.shape[3:])

    n_in = jnp.arange(S5_STEP - 1, -1, -1)
    wk_c = by_block(jnp.stack([anb_re[n_in], anb_im[n_in]]))
    wk_c = wk_c.transpose(2, 1, 3, 5, 0, 4).reshape(S5_BLOCKS, S5_STEP * LANES, 2 * S5_STATE)
    wk = jnp.einsum('jrs,st->jrt', wk_c, e_state, precision=hi) * same_state
    n_out = jnp.arange(1, S5_STEP + 1)
    wo_c = by_block(jnp.stack([can_re[n_out], -can_im[n_out]]))
    wo_c = wo_c.transpose(2, 0, 5, 1, 3, 4).reshape(S5_BLOCKS, 2 * S5_STATE, S5_STEP * LANES)
    wo = jnp.einsum('ts,jsr->jtr', e_state.T, wo_c, precision=hi) * same_state.T
    t_n = (jnp.einsum('ngop,gpc->ngco', can_re, bb_re, precision=hi)
           - jnp.einsum('ngop,gpc->ngco', can_im, bb_im, precision=hi))
    lag = np.arange(S5_STEP)[None, :] - np.arange(S5_STEP)[:, None]
    tz_c = jnp.where((lag >= 0)[:, :, None, None, None], t_n[np.maximum(lag, 0)], 0.0)
    tz_c = by_block(tz_c).transpose(2, 0, 3, 4, 1, 5).reshape(S5_BLOCKS, S5_STEP * LANES, S5_STEP * S5_GROUP)
    tz = jnp.einsum('jrs,st->jrt', tz_c, e_chan, precision=hi) * same_chan

    def rep(a):
        return jnp.broadcast_to(a.reshape(S5_BLOCKS, 1, S5_BLOCK_STATE), (S5_BLOCKS, SUBLANES, S5_BLOCK_STATE))
    return wk.astype(BF16), rep(p_re[S5_STEP]), rep(p_im[S5_STEP]), wo.astype(BF16), tz.astype(BF16)


def kernel(x, norm_g, w_in, s5_lam_re, s5_lam_im, s5_log_dt, s5_b_re, s5_b_im, s5_c_re, s5_c_im, s5_d, s5_w_glu, s5_b_glu, pool_w, pool_scale, sgu_ln_g, sgu_ln_b, sgu_w, sgu_b, w_branch, w_out, final_norm_g):
    bsz, seq, d = x.shape
    assert (bsz, seq, d) == (BATCH, SEQ, D_MODEL)
    wk, a_re, a_im, wo, tz = jax.vmap(_s5_params)(s5_lam_re, s5_lam_im, s5_log_dt, s5_b_re, s5_b_im,
                                                  s5_c_re, s5_c_im)
    params = (
        norm_g.reshape(DEPTH, 1, d), w_in.astype(BF16), wk, a_re, a_im, wo, tz,
        s5_d.reshape(DEPTH, 1, D_BRANCH), s5_w_glu.astype(BF16), s5_b_glu.reshape(DEPTH, 1, D_BRANCH),
        pool_w.astype(BF16), pool_scale.reshape(DEPTH, 1, D_BRANCH),
        sgu_ln_g.reshape(DEPTH, 1, D_BRANCH), sgu_ln_b.reshape(DEPTH, 1, D_BRANCH),
        sgu_w, sgu_b.reshape(DEPTH, SGU_HEADS, SGU_BLOCK, 1),
        w_branch.astype(BF16), w_out.astype(BF16), final_norm_g.reshape(1, 1, d),
    )
    for l in range(DEPTH):
        x = _layer_call(x, params, l, first=(l == 0), final=(l == DEPTH - 1))
    return x
```

```python
import functools

import jax
import jax.numpy as jnp
import numpy as np
from jax import lax
from jax.experimental import pallas as pl
from jax.experimental.pallas import tpu as pltpu

D_MODEL = 1024
BATCH = 16
SEQ = 2048
DEPTH = 2
CHUNK = 64
D_BRANCH = 512
N_BRANCH = 3
S5_GROUP = 16
S5_GROUPS = D_BRANCH // S5_GROUP
S5_STATE = 64
POOL_WINDOWS = (2, 4, 8, 16)
POOL_GROUP = D_BRANCH // len(POOL_WINDOWS)
SGU_BLOCK = 128
SGU_HEADS = 4
SGU_HEAD_DIM = D_BRANCH // SGU_HEADS
RMS_EPS = 1e-6
LN_EPS = 1e-5

LANES = 128
SUBLANES = 8
S5_BLOCKS = D_BRANCH // LANES
S5_BLOCK_STATE = (LANES // S5_GROUP) * S5_STATE
T_TILE = CHUNK
ROWS = T_TILE * BATCH
ROW_CHUNK = 512
N_CHUNKS = ROWS // ROW_CHUNK
T_CHUNK = ROW_CHUNK // BATCH
HALO_ROWS = max(POOL_WINDOWS) * BATCH
N_SUB = BATCH // SUBLANES
S5_STEP = 2
VMEM_LIMIT_BYTES = 62 * 1024 * 1024

OFF_A_VAL, OFF_A_GATE, OFF_B_VAL, OFF_B_GATE, OFF_C_U, OFF_C_V, OFF_C_GATE, OFF_GATES = (
    0, 512, 1024, 1536, 2048, 2560, 3072, 3584)

F32 = jnp.float32
BF16 = jnp.bfloat16


def _rows(c, n=ROW_CHUNK, base=0):
    if isinstance(c, int):
        return pl.ds(base + c * n, n)
    return pl.ds(pl.multiple_of(base + c * n, SUBLANES), n)


def _z_row(c, t, bh):
    tb, tl = divmod(t, SUBLANES)
    static = (tb * BATCH + bh * SUBLANES) * SUBLANES + tl
    return c * (T_CHUNK * BATCH) + static


def _dot(a, b):
    return jnp.dot(a, b, preferred_element_type=F32)


def _for_chunks(body):
    for c in range(N_CHUNKS):
        body(c)


def _tile_copies(hbm_ref, buf_ref, sem_ref, tile, slot, to_hbm):
    copies = []
    for b in range(BATCH):
        hbm = hbm_ref.at[b, pl.ds(tile * T_TILE, T_TILE), :]
        vmem = buf_ref.at[slot, :, b, :]
        src, dst = (vmem, hbm) if to_hbm else (hbm, vmem)
        copies.append(pltpu.make_async_copy(src, dst, sem_ref.at[slot, b]))
    return copies


def _layer_kernel(x_ref, g_ref, win_ref, wkc_ref, are_ref, aim_ref, woc_ref, tzc_ref, dsk_ref, wglu_ref, bglu_ref,
                  pw_ref, psc_ref, lng_ref, lnb_ref, sw_ref, sb_ref, wbr_ref, wout_ref, fg_ref,
                  es_ref, est_ref, ec_ref, ms_ref, mst_ref, mc_ref,
                  o_ref,
                  h_scr, ext_scr, st_scr, vn_scr, vsave_scr, z_scr, ya_scr, wk_ref, wo_ref, tz_ref, *io_scr,
                  first, final):
    i = pl.program_id(0)
    n_tiles = pl.num_programs(0)
    slot = i % 2
    half = S5_BLOCK_STATE
    if first:
        xbuf, xsem = io_scr
    if final:
        obuf, osem = io_scr

    @pl.when(i == 0)
    def _():
        st_scr[...] = jnp.zeros_like(st_scr)
        ext_scr[0:HALO_ROWS, :] = jnp.zeros((HALO_ROWS, D_BRANCH), F32)
        vsave_scr[...] = jnp.zeros_like(vsave_scr)
        for j in range(S5_BLOCKS):
            wk_ref[j] = (_dot(wkc_ref[j], es_ref[...]) * ms_ref[...]).astype(BF16)
            wo_ref[j] = (_dot(est_ref[...], woc_ref[j]) * mst_ref[...]).astype(BF16)
            tz_ref[j] = (_dot(tzc_ref[j], ec_ref[...]) * mc_ref[...]).astype(BF16)

    if first:
        @pl.when(i == 0)
        def _():
            for cp in _tile_copies(x_ref, xbuf, xsem, 0, 0, to_hbm=False):
                cp.start()

        @pl.when(i + 1 < n_tiles)
        def _():
            for cp in _tile_copies(x_ref, xbuf, xsem, i + 1, 1 - slot, to_hbm=False):
                cp.start()

        for cp in _tile_copies(x_ref, xbuf, xsem, i, slot, to_hbm=False):
            cp.wait()

    if final:
        @pl.when(i >= 2)
        def _():
            for cp in _tile_copies(o_ref, obuf, osem, i - 2, slot, to_hbm=True):
                cp.wait()

    def x_rows(c):
        if first:
            return xbuf[slot, pl.ds(c * T_CHUNK, T_CHUNK), :, :].reshape(ROW_CHUNK, D_MODEL)
        return x_ref[_rows(c), :]

    def loop_a(c):
        r = _rows(c)
        xc = x_rows(c)
        ms = jnp.mean(xc * xc, axis=-1, keepdims=True)
        hc = (xc * lax.rsqrt(ms + RMS_EPS) * g_ref[...]).astype(BF16)
        h_scr[r, :] = hc

        def project_pool_value():
            ext_scr[_rows(c, base=HALO_ROWS), :] = _dot(hc, win_ref[:, OFF_B_VAL:OFF_B_VAL + D_BRANCH])

        def project_sgu_value():
            v = _dot(hc, win_ref[:, OFF_C_V:OFF_C_V + D_BRANCH])
            mu = jnp.mean(v, axis=-1, keepdims=True)
            vc = v - mu
            var = jnp.mean(vc * vc, axis=-1, keepdims=True)
            vn = vc * lax.rsqrt(var + LN_EPS) * lng_ref[...] + lnb_ref[...]
            for hd in range(SGU_HEADS):
                for bh in range(N_SUB):
                    slab = jnp.concatenate(
                        [vn[t * BATCH + bh * SUBLANES:t * BATCH + (bh + 1) * SUBLANES,
                            hd * SGU_HEAD_DIM:(hd + 1) * SGU_HEAD_DIM] for t in range(T_CHUNK)], axis=0)
                    vn_scr[hd, _rows(c, n=T_CHUNK * SUBLANES, base=bh * T_TILE * SUBLANES), :] = slab

        gate = []

        def project_s5_gate():
            gate.append(_dot(hc, win_ref[:, OFF_A_GATE:OFF_A_GATE + D_BRANCH]))

        independent = [project_pool_value, project_sgu_value, project_s5_gate]

        aval = _dot(hc, win_ref[:, OFF_A_VAL:OFF_A_VAL + D_BRANCH])
        n_blk = T_CHUNK // S5_STEP
        ys = []
        for j in range(S5_BLOCKS):
            u = aval[:, j * LANES:(j + 1) * LANES].astype(BF16)
            ub = jnp.concatenate(
                [jnp.concatenate([u[(S5_STEP * blk + i) * BATCH:(S5_STEP * blk + i + 1) * BATCH, :]
                                  for blk in range(n_blk)], axis=0) for i in range(S5_STEP)], axis=1)
            cin = _dot(ub, wk_ref[j])
            if independent:
                independent.pop(0)()
            a_re, a_im = are_ref[j], aim_ref[j]
            state = []
            for s in range(N_SUB):
                rs = slice(SUBLANES * s, SUBLANES * (s + 1))
                state += [st_scr[j, rs, 0:half], st_scr[j, rs, half:2 * half]]
            hp_rows = []
            for blk in range(n_blk):
                for s in range(N_SUB):
                    h_re, h_im = state[2 * s], state[2 * s + 1]
                    hp_rows.append(jnp.concatenate([h_re, h_im], axis=1))
                    b = cin[blk * BATCH + s * SUBLANES:blk * BATCH + (s + 1) * SUBLANES, :]
                    state[2 * s] = a_re * h_re - a_im * h_im + b[:, :half]
                    state[2 * s + 1] = a_re * h_im + a_im * h_re + b[:, half:]
            for s in range(N_SUB):
                rs = slice(SUBLANES * s, SUBLANES * (s + 1))
                st_scr[j, rs, 0:half] = state[2 * s]
                st_scr[j, rs, half:2 * half] = state[2 * s + 1]
            hp = jnp.concatenate(hp_rows, axis=0).astype(BF16)
            yb = _dot(hp, wo_ref[j]) + _dot(ub, tz_ref[j])
            ys.append(jnp.concatenate(
                [yb[blk * BATCH:(blk + 1) * BATCH, i * LANES:(i + 1) * LANES]
                 for blk in range(n_blk) for i in range(S5_STEP)], axis=0))
        while independent:
            independent.pop(0)()
        y = jnp.concatenate(ys, axis=1) + dsk_ref[...] * aval
        y = jax.nn.gelu(y)
        y = y * jax.nn.sigmoid(_dot(y.astype(BF16), wglu_ref[...]) + bglu_ref[...])
        ya_scr[r, :] = (y * jax.nn.silu(gate[0])).astype(BF16)
    _for_chunks(loop_a)

    second = (i % 2) == 1
    t0 = pl.multiple_of((i % 2) * CHUNK, CHUNK)
    for hd in range(SGU_HEADS):
        w_rows = sw_ref[hd, pl.ds(t0, CHUNK), :]
        w_lo, w_hi = w_rows[:, 0:CHUNK], w_rows[:, CHUNK:2 * CHUNK]
        w_cur = jnp.where(second, w_hi, w_lo).astype(BF16)
        w_prev = jnp.where(second, w_lo, 0.0).astype(BF16)
        v_cur = jnp.concatenate(
            [vn_scr[hd, pl.ds((b // SUBLANES) * T_TILE * SUBLANES + b % SUBLANES, CHUNK, stride=SUBLANES), :]
             for b in range(BATCH)], axis=1).astype(BF16)
        zt = _dot(w_cur, v_cur) + _dot(w_prev, vsave_scr[hd]) + sb_ref[hd, pl.ds(t0, CHUNK), :]
        vsave_scr[hd] = v_cur
        for tb in range(CHUNK // SUBLANES):
            z_scr[hd, tb * BATCH * SUBLANES:(tb + 1) * BATCH * SUBLANES, :] = jnp.concatenate(
                [zt[tb * SUBLANES:(tb + 1) * SUBLANES, b * SGU_HEAD_DIM:(b + 1) * SGU_HEAD_DIM]
                 for b in range(BATCH)], axis=0)

    def loop_b(c):
        r = _rows(c)
        hc = h_scr[r, :]

        t_idx = lax.broadcasted_iota(jnp.int32, (ROW_CHUNK, 1), 0) // BATCH
        pos = (i * T_TILE + c * T_CHUNK + 1 + t_idx).astype(F32)
        ys = []
        for gi, w in enumerate(POOL_WINDOWS):
            cols = slice(gi * POOL_GROUP, (gi + 1) * POOL_GROUP)
            cur = ext_scr[_rows(c, base=HALO_ROWS), cols]
            acc = cur
            for k in range(1, w):
                acc = acc + ext_scr[_rows(c, base=HALO_ROWS - k * BATCH), cols]
            p = acc / jnp.minimum(pos, float(w)) - cur
            ys.append(_dot(p.astype(BF16), pw_ref[gi]))
        y = jnp.concatenate(ys, axis=1) * psc_ref[...]
        gate = _dot(hc, win_ref[:, OFF_B_GATE:OFF_B_GATE + D_BRANCH])
        yb = (y * jax.nn.silu(gate)).astype(BF16)

        z = jnp.concatenate(
            [jnp.concatenate(
                [z_scr[hd, pl.ds(_z_row(c, t, bh), SUBLANES, stride=SUBLANES), :]
                 for t in range(T_CHUNK) for bh in range(N_SUB)], axis=0)
             for hd in range(SGU_HEADS)], axis=1)
        u = _dot(hc, win_ref[:, OFF_C_U:OFF_C_U + D_BRANCH])
        gate = _dot(hc, win_ref[:, OFF_C_GATE:OFF_C_GATE + D_BRANCH])
        yc = (u * z * jax.nn.silu(gate)).astype(BF16)

        merged = None
        for k, yk in enumerate((ya_scr[r, :], yb, yc)):
            mg = _dot(hc, win_ref[:, OFF_GATES + k * D_MODEL:OFF_GATES + (k + 1) * D_MODEL])
            term = jax.nn.sigmoid(mg) * _dot(yk, wbr_ref[k])
            merged = term if merged is None else merged + term
        out = x_rows(c) + _dot(merged.astype(BF16), wout_ref[...])
        if final:
            ms = jnp.mean(out * out, axis=-1, keepdims=True)
            out = out * lax.rsqrt(ms + RMS_EPS) * fg_ref[...]
            obuf[slot, pl.ds(c * T_CHUNK, T_CHUNK), :, :] = out.reshape(T_CHUNK, BATCH, D_MODEL)
        else:
            o_ref[r, :] = out
    _for_chunks(loop_b)

    ext_scr[0:HALO_ROWS, :] = ext_scr[ROWS:ROWS + HALO_ROWS, :]

    if final:
        for cp in _tile_copies(o_ref, obuf, osem, i, slot, to_hbm=True):
            cp.start()

        @pl.when(i == n_tiles - 1)
        def _():
            for cp in _tile_copies(o_ref, obuf, osem, i - 1, 1 - slot, to_hbm=True):
                cp.wait()
            for cp in _tile_copies(o_ref, obuf, osem, i, slot, to_hbm=True):
                cp.wait()


def _layer_spec(shape, layer):
    nd = len(shape)
    return pl.BlockSpec((None,) + tuple(shape[1:]), lambda i, _l=layer, _nd=nd: (_l,) + (0,) * (_nd - 1),
                        pipeline_mode=pl.Buffered(1))


def _layer_call(x, params, layer, first, final):
    in_hbm = pl.BlockSpec(memory_space=pl.ANY)
    time_major = pl.BlockSpec((ROWS, D_MODEL), lambda i: (i, 0))
    in_specs = [in_hbm if first else time_major] + [
        _layer_spec(p.shape, min(layer, p.shape[0] - 1)) for p in params]
    scratch = [
        pltpu.VMEM((ROWS, D_MODEL), BF16),
        pltpu.VMEM((HALO_ROWS + ROWS, D_BRANCH), F32),
        pltpu.VMEM((S5_BLOCKS, BATCH, 2 * S5_BLOCK_STATE), F32),
        pltpu.VMEM((SGU_HEADS, ROWS, SGU_HEAD_DIM), F32),
        pltpu.VMEM((SGU_HEADS, CHUNK, BATCH * SGU_HEAD_DIM), BF16),
        pltpu.VMEM((SGU_HEADS, ROWS, SGU_HEAD_DIM), F32),
        pltpu.VMEM((ROWS, D_BRANCH), BF16),
        pltpu.VMEM((S5_BLOCKS, S5_STEP * LANES, 2 * S5_BLOCK_STATE), BF16),
        pltpu.VMEM((S5_BLOCKS, 2 * S5_BLOCK_STATE, S5_STEP * LANES), BF16),
        pltpu.VMEM((S5_BLOCKS, S5_STEP * LANES, S5_STEP * LANES), BF16),
    ]
    assert SEQ // T_TILE >= 2 and not (first and final)
    io_buffer = [pltpu.VMEM((2, T_TILE, BATCH, D_MODEL), F32), pltpu.SemaphoreType.DMA((2, BATCH))]
    if first or final:
        scratch += io_buffer
    return pl.pallas_call(
        functools.partial(_layer_kernel, first=first, final=final),
        grid=(SEQ // T_TILE,),
        in_specs=in_specs,
        out_specs=in_hbm if final else time_major,
        out_shape=jax.ShapeDtypeStruct((BATCH, SEQ, D_MODEL) if final else (SEQ * BATCH, D_MODEL), F32),
        scratch_shapes=scratch,
        compiler_params=pltpu.CompilerParams(
            dimension_semantics=("arbitrary",), vmem_limit_bytes=VMEM_LIMIT_BYTES),
        name="trunk_layer_final" if final else "trunk_layer",
    )(x, *params)


def _expansion(n_compact, n_full, compact_of_full):
    e = np.zeros((n_compact, n_full), np.float32)
    e[compact_of_full, np.arange(n_full)] = 1.0
    return e


def _s5_constants():
    gpb = LANES // S5_GROUP
    q, h, p = np.meshgrid(np.arange(2), np.arange(gpb), np.arange(S5_STATE), indexing='ij')
    e_state = _expansion(2 * S5_STATE, 2 * S5_BLOCK_STATE, (q * S5_STATE + p).ravel())
    i, g, c = np.meshgrid(np.arange(S5_STEP), np.arange(gpb), np.arange(S5_GROUP), indexing='ij')
    e_chan = _expansion(S5_STEP * S5_GROUP, S5_STEP * LANES, (i * S5_GROUP + c).ravel())
    same_state = (g.ravel()[:, None] == h.ravel()[None, :]).astype(np.float32)
    same_chan = (g.ravel()[:, None] == g.ravel()[None, :]).astype(np.float32)
    return e_state, e_chan, same_state, same_chan


def _s5_params(lam_re, lam_im, log_dt, b_re, b_im, c_re, c_im):
    hi = lax.Precision.HIGHEST
    dt = jnp.exp(log_dt)[:, None]
    mag = jnp.exp(lam_re * dt)
    ab_re = mag * jnp.cos(lam_im * dt)
    ab_im = mag * jnp.sin(lam_im * dt)
    den = lam_re * lam_re + lam_im * lam_im
    nr = ab_re - 1.0
    ni = ab_im
    k_re = (nr * lam_re + ni * lam_im) / den
    k_im = (ni * lam_re - nr * lam_im) / den
    bb_re = k_re[..., None] * b_re - k_im[..., None] * b_im
    bb_im = k_re[..., None] * b_im + k_im[..., None] * b_re
    gpb = LANES // S5_GROUP

    pw = [(jnp.ones_like(ab_re), jnp.zeros_like(ab_im))]
    for _ in range(S5_STEP):
        p_re, p_im = pw[-1]
        pw.append((p_re * ab_re - p_im * ab_im, p_re * ab_im + p_im * ab_re))
    p_re = jnp.stack([p[0] for p in pw])
    p_im = jnp.stack([p[1] for p in pw])

    anb_re = p_re[..., None] * bb_re - p_im[..., None] * bb_im
    anb_im = p_re[..., None] * bb_im + p_im[..., None] * bb_re
    can_re = c_re * p_re[:, :, None, :] - c_im * p_im[:, :, None, :]
    can_im = c_re * p_im[:, :, None, :] + c_im * p_re[:, :, None, :]

    def by_block(m):
        return m.reshape(m.shape[:2] + (S5_BLOCKS, gpb) + m.shape[3:])

    n_in = jnp.arange(S5_STEP - 1, -1, -1)
    wk_c = by_block(jnp.stack([anb_re[n_in], anb_im[n_in]]))
    wk_c = wk_c.transpose(2, 1, 3, 5, 0, 4).reshape(S5_BLOCKS, S5_STEP * LANES, 2 * S5_STATE)
    n_out = jnp.arange(1, S5_STEP + 1)
    wo_c = by_block(jnp.stack([can_re[n_out], -can_im[n_out]]))
    wo_c = wo_c.transpose(2, 0, 5, 1, 3, 4).reshape(S5_BLOCKS, 2 * S5_STATE, S5_STEP * LANES)
    t_n = (jnp.einsum('ngop,gpc->ngco', can_re, bb_re, precision=hi)
           - jnp.einsum('ngop,gpc->ngco', can_im, bb_im, precision=hi))
    lag = np.arange(S5_STEP)[None, :] - np.arange(S5_STEP)[:, None]
    tz_c = jnp.where((lag >= 0)[:, :, None, None, None], t_n[np.maximum(lag, 0)], 0.0)
    tz_c = by_block(tz_c).transpose(2, 0, 3, 4, 1, 5).reshape(S5_BLOCKS, S5_STEP * LANES, S5_STEP * S5_GROUP)

    def rep(a):
        return jnp.broadcast_to(a.reshape(S5_BLOCKS, 1, S5_BLOCK_STATE), (S5_BLOCKS, SUBLANES, S5_BLOCK_STATE))
    return wk_c.astype(BF16), rep(p_re[S5_STEP]), rep(p_im[S5_STEP]), wo_c.astype(BF16), tz_c.astype(BF16)


def kernel(x, norm_g, w_in, s5_lam_re, s5_lam_im, s5_log_dt, s5_b_re, s5_b_im, s5_c_re, s5_c_im, s5_d, s5_w_glu, s5_b_glu, pool_w, pool_scale, sgu_ln_g, sgu_ln_b, sgu_w, sgu_b, w_branch, w_out, final_norm_g):
    bsz, seq, d = x.shape
    assert (bsz, seq, d) == (BATCH, SEQ, D_MODEL)
    wk, a_re, a_im, wo, tz = jax.vmap(_s5_params)(s5_lam_re, s5_lam_im, s5_log_dt, s5_b_re, s5_b_im,
                                                  s5_c_re, s5_c_im)
    params = (
        norm_g.reshape(DEPTH, 1, d), w_in.astype(BF16), wk, a_re, a_im, wo, tz,
        s5_d.reshape(DEPTH, 1, D_BRANCH), s5_w_glu.astype(BF16), s5_b_glu.reshape(DEPTH, 1, D_BRANCH),
        pool_w.astype(BF16), pool_scale.reshape(DEPTH, 1, D_BRANCH),
        sgu_ln_g.reshape(DEPTH, 1, D_BRANCH), sgu_ln_b.reshape(DEPTH, 1, D_BRANCH),
        sgu_w, sgu_b.reshape(DEPTH, SGU_HEADS, SGU_BLOCK, 1),
        w_branch.astype(BF16), w_out.astype(BF16), final_norm_g.reshape(1, 1, d),
    )
    e_state, e_chan, same_state, same_chan = _s5_constants()
    params += tuple(jnp.asarray(c[None], BF16) for c in (
        e_state, e_state.T, e_chan, same_state, same_state.T, same_chan))
    for l in range(DEPTH):
        x = _layer_call(x, params, l, first=(l == 0), final=(l == DEPTH - 1))
    return x
```

```python
import functools

import jax
import jax.numpy as jnp
import numpy as np
from jax import lax
from jax.experimental import pallas as pl
from jax.experimental.pallas import tpu as pltpu

D_MODEL = 1024
BATCH = 16
SEQ = 2048
DEPTH = 2
CHUNK = 64
D_BRANCH = 512
N_BRANCH = 3
S5_GROUP = 16
S5_GROUPS = D_BRANCH // S5_GROUP
S5_STATE = 64
POOL_WINDOWS = (2, 4, 8, 16)
POOL_GROUP = D_BRANCH // len(POOL_WINDOWS)
SGU_BLOCK = 128
SGU_HEADS = 4
SGU_HEAD_DIM = D_BRANCH // SGU_HEADS
RMS_EPS = 1e-6
LN_EPS = 1e-5

LANES = 128
SUBLANES = 8
S5_BLOCKS = D_BRANCH // LANES
S5_BLOCK_STATE = (LANES // S5_GROUP) * S5_STATE
T_TILE = CHUNK
ROWS = T_TILE * BATCH
ROW_CHUNK = 512
N_CHUNKS = ROWS // ROW_CHUNK
T_CHUNK = ROW_CHUNK // BATCH
HALO_ROWS = max(POOL_WINDOWS) * BATCH
N_SUB = BATCH // SUBLANES
S5_STEP = 2
VMEM_LIMIT_BYTES = 62 * 1024 * 1024

OFF_A_VAL, OFF_A_GATE, OFF_B_VAL, OFF_B_GATE, OFF_C_U, OFF_C_V, OFF_C_GATE, OFF_GATES = (
    0, 512, 1024, 1536, 2048, 2560, 3072, 3584)

F32 = jnp.float32
BF16 = jnp.bfloat16


def _rows(c, n=ROW_CHUNK, base=0):
    if isinstance(c, int):
        return pl.ds(base + c * n, n)
    return pl.ds(pl.multiple_of(base + c * n, SUBLANES), n)


def _z_row(c, t, bh):
    tb, tl = divmod(t, SUBLANES)
    static = (tb * BATCH + bh * SUBLANES) * SUBLANES + tl
    return c * (T_CHUNK * BATCH) + static


def _dot(a, b):
    return jnp.dot(a, b, preferred_element_type=F32)


def _for_chunks(body):
    for c in range(N_CHUNKS):
        body(c)


def _tile_copies(hbm_ref, buf_ref, sem_ref, tile, slot, to_hbm):
    copies = []
    for b in range(BATCH):
        hbm = hbm_ref.at[b, pl.ds(tile * T_TILE, T_TILE), :]
        vmem = buf_ref.at[slot, :, b, :]
        src, dst = (vmem, hbm) if to_hbm else (hbm, vmem)
        copies.append(pltpu.make_async_copy(src, dst, sem_ref.at[slot, b]))
    return copies


def _layer_kernel(x_ref, g_ref, win_ref, wkc_ref, are_ref, aim_ref, woc_ref, tzc_ref, dsk_ref, wglu_ref, bglu_ref,
                  pw_ref, psc_ref, lng_ref, lnb_ref, sw_ref, sb_ref, wbr_ref, wout_ref, fg_ref,
                  es_ref, est_ref, ec_ref, ms_ref, mst_ref, mc_ref,
                  o_ref,
                  h_scr, ext_scr, st_scr, vn_scr, vsave_scr, z_scr, ya_scr, wk_ref, wo_ref, tz_ref, *io_scr,
                  first, final):
    i = pl.program_id(0)
    n_tiles = pl.num_programs(0)
    slot = i % 2
    half = S5_BLOCK_STATE
    if first:
        xbuf, xsem = io_scr
    if final:
        obuf, osem = io_scr

    @pl.when(i == 0)
    def _():
        st_scr[...] = jnp.zeros_like(st_scr)
        ext_scr[0:HALO_ROWS, :] = jnp.zeros((HALO_ROWS, D_BRANCH), F32)
        vsave_scr[...] = jnp.zeros_like(vsave_scr)
        for j in range(S5_BLOCKS):
            wk_ref[j] = (_dot(wkc_ref[j], es_ref[...]) * ms_ref[...]).astype(BF16)
            wo_ref[j] = (_dot(est_ref[...], woc_ref[j]) * mst_ref[...]).astype(BF16)
            tz_ref[j] = (_dot(tzc_ref[j], ec_ref[...]) * mc_ref[...]).astype(BF16)

    if first:
        @pl.when(i == 0)
        def _():
            for cp in _tile_copies(x_ref, xbuf, xsem, 0, 0, to_hbm=False):
                cp.start()

        @pl.when(i + 1 < n_tiles)
        def _():
            for cp in _tile_copies(x_ref, xbuf, xsem, i + 1, 1 - slot, to_hbm=False):
                cp.start()

        for cp in _tile_copies(x_ref, xbuf, xsem, i, slot, to_hbm=False):
            cp.wait()

    if final:
        @pl.when(i >= 2)
        def _():
            for cp in _tile_copies(o_ref, obuf, osem, i - 2, slot, to_hbm=True):
                cp.wait()

    def x_rows(c):
        if first:
            return xbuf[slot, pl.ds(c * T_CHUNK, T_CHUNK), :, :].reshape(ROW_CHUNK, D_MODEL)
        return x_ref[_rows(c), :]

    def loop_a(c):
        r = _rows(c)
        xc = x_rows(c)
        ms = jnp.mean(xc * xc, axis=-1, keepdims=True)
        hc = (xc * lax.rsqrt(ms + RMS_EPS) * g_ref[...]).astype(BF16)
        h_scr[r, :] = hc

        def project_pool_value():
            ext_scr[_rows(c, base=HALO_ROWS), :] = _dot(hc, win_ref[:, OFF_B_VAL:OFF_B_VAL + D_BRANCH])

        def project_sgu_value():
            v = _dot(hc, win_ref[:, OFF_C_V:OFF_C_V + D_BRANCH])
            mu = jnp.mean(v, axis=-1, keepdims=True)
            vc = v - mu
            var = jnp.mean(vc * vc, axis=-1, keepdims=True)
            vn = vc * lax.rsqrt(var + LN_EPS) * lng_ref[...] + lnb_ref[...]
            for hd in range(SGU_HEADS):
                for bh in range(N_SUB):
                    slab = jnp.concatenate(
                        [vn[t * BATCH + bh * SUBLANES:t * BATCH + (bh + 1) * SUBLANES,
                            hd * SGU_HEAD_DIM:(hd + 1) * SGU_HEAD_DIM] for t in range(T_CHUNK)], axis=0)
                    vn_scr[hd, _rows(c, n=T_CHUNK * SUBLANES, base=bh * T_TILE * SUBLANES), :] = slab

        gate = []

        def project_s5_gate():
            gate.append(_dot(hc, win_ref[:, OFF_A_GATE:OFF_A_GATE + D_BRANCH]))

        independent = [project_pool_value, project_sgu_value, project_s5_gate]

        aval = _dot(hc, win_ref[:, OFF_A_VAL:OFF_A_VAL + D_BRANCH])
        n_blk = T_CHUNK // S5_STEP
        ys = []
        for j in range(S5_BLOCKS):
            u = aval[:, j * LANES:(j + 1) * LANES].astype(BF16)
            ub = jnp.concatenate(
                [jnp.concatenate([u[(S5_STEP * blk + i) * BATCH:(S5_STEP * blk + i + 1) * BATCH, :]
                                  for blk in range(n_blk)], axis=0) for i in range(S5_STEP)], axis=1)
            cin = _dot(ub, wk_ref[j])
            if independent:
                independent.pop(0)()
            a_re, a_im = are_ref[j], aim_ref[j]
            state = []
            for s in range(N_SUB):
                rs = slice(SUBLANES * s, SUBLANES * (s + 1))
                state += [st_scr[j, rs, 0:half], st_scr[j, rs, half:2 * half]]
            hp_rows = []
            for blk in range(n_blk):
                for s in range(N_SUB):
                    h_re, h_im = state[2 * s], state[2 * s + 1]
                    hp_rows.append(jnp.concatenate([h_re, h_im], axis=1))
                    b = cin[blk * BATCH + s * SUBLANES:blk * BATCH + (s + 1) * SUBLANES, :]
                    state[2 * s] = a_re * h_re - a_im * h_im + b[:, :half]
                    state[2 * s + 1] = a_re * h_im + a_im * h_re + b[:, half:]
            for s in range(N_SUB):
                rs = slice(SUBLANES * s, SUBLANES * (s + 1))
                st_scr[j, rs, 0:half] = state[2 * s]
                st_scr[j, rs, half:2 * half] = state[2 * s + 1]
            hp = jnp.concatenate(hp_rows, axis=0).astype(BF16)
            yb = _dot(hp, wo_ref[j]) + _dot(ub, tz_ref[j])
            ys.append(jnp.concatenate(
                [yb[blk * BATCH:(blk + 1) * BATCH, i * LANES:(i + 1) * LANES]
                 for blk in range(n_blk) for i in range(S5_STEP)], axis=0))
        while independent:
            independent.pop(0)()
        y = jnp.concatenate(ys, axis=1) + dsk_ref[...] * aval
        y = jax.nn.gelu(y)
        y = y * jax.nn.sigmoid(_dot(y.astype(BF16), wglu_ref[...]) + bglu_ref[...])
        ya_scr[r, :] = (y * jax.nn.silu(gate[0])).astype(BF16)
    _for_chunks(loop_a)

    second = (i % 2) == 1
    t0 = pl.multiple_of((i % 2) * CHUNK, CHUNK)
    for hd in range(SGU_HEADS):
        w_rows = sw_ref[hd, pl.ds(t0, CHUNK), :]
        w_lo, w_hi = w_rows[:, 0:CHUNK], w_rows[:, CHUNK:2 * CHUNK]
        w_cur = jnp.where(second, w_hi, w_lo).astype(BF16)
        w_prev = jnp.where(second, w_lo, 0.0).astype(BF16)
        v_cur = jnp.concatenate(
            [vn_scr[hd, pl.ds((b // SUBLANES) * T_TILE * SUBLANES + b % SUBLANES, CHUNK, stride=SUBLANES), :]
             for b in range(BATCH)], axis=1).astype(BF16)
        zt = _dot(w_cur, v_cur) + _dot(w_prev, vsave_scr[hd]) + sb_ref[hd, pl.ds(t0, CHUNK), :]
        vsave_scr[hd] = v_cur
        for tb in range(CHUNK // SUBLANES):
            z_scr[hd, tb * BATCH * SUBLANES:(tb + 1) * BATCH * SUBLANES, :] = jnp.concatenate(
                [zt[tb * SUBLANES:(tb + 1) * SUBLANES, b * SGU_HEAD_DIM:(b + 1) * SGU_HEAD_DIM]
                 for b in range(BATCH)], axis=0)

    def loop_b(c):
        r = _rows(c)
        hc = h_scr[r, :]

        t_idx = lax.broadcasted_iota(jnp.int32, (ROW_CHUNK, 1), 0) // BATCH
        pos = (i * T_TILE + c * T_CHUNK + 1 + t_idx).astype(F32)
        ys = []
        for gi, w in enumerate(POOL_WINDOWS):
            cols = slice(gi * POOL_GROUP, (gi + 1) * POOL_GROUP)
            cur = ext_scr[_rows(c, base=HALO_ROWS), cols]
            acc = cur
            for k in range(1, w):
                acc = acc + ext_scr[_rows(c, base=HALO_ROWS - k * BATCH), cols]
            p = acc / jnp.minimum(pos, float(w)) - cur
            ys.append(_dot(p.astype(BF16), pw_ref[gi]))
        y = jnp.concatenate(ys, axis=1) * psc_ref[...]
        gate = _dot(hc, win_ref[:, OFF_B_GATE:OFF_B_GATE + D_BRANCH])
        yb = (y * jax.nn.silu(gate)).astype(BF16)

        z = jnp.concatenate(
            [jnp.concatenate(
                [z_scr[hd, pl.ds(_z_row(c, t, bh), SUBLANES, stride=SUBLANES), :]
                 for t in range(T_CHUNK) for bh in range(N_SUB)], axis=0)
             for hd in range(SGU_HEADS)], axis=1)
        u = _dot(hc, win_ref[:, OFF_C_U:OFF_C_U + D_BRANCH])
        gate = _dot(hc, win_ref[:, OFF_C_GATE:OFF_C_GATE + D_BRANCH])
        yc = (u * z * jax.nn.silu(gate)).astype(BF16)

        merged = None
        for k, yk in enumerate((ya_scr[r, :], yb, yc)):
            mg = _dot(hc, win_ref[:, OFF_GATES + k * D_MODEL:OFF_GATES + (k + 1) * D_MODEL])
            term = jax.nn.sigmoid(mg) * _dot(yk, wbr_ref[k])
            merged = term if merged is None else merged + term
        out = x_rows(c) + _dot(merged.astype(BF16), wout_ref[...])
        if final:
            ms = jnp.mean(out * out, axis=-1, keepdims=True)
            out = out * lax.rsqrt(ms + RMS_EPS) * fg_ref[...]
            obuf[slot, pl.ds(c * T_CHUNK, T_CHUNK), :, :] = out.reshape(T_CHUNK, BATCH, D_MODEL)
        else:
            o_ref[r, :] = out
    _for_chunks(loop_b)

    ext_scr[0:HALO_ROWS, :] = ext_scr[ROWS:ROWS + HALO_ROWS, :]

    if final:
        for cp in _tile_copies(o_ref, obuf, osem, i, slot, to_hbm=True):
            cp.start()

        @pl.when(i == n_tiles - 1)
        def _():
            for cp in _tile_copies(o_ref, obuf, osem, i - 1, 1 - slot, to_hbm=True):
                cp.wait()
            for cp in _tile_copies(o_ref, obuf, osem, i, slot, to_hbm=True):
                cp.wait()


def _layer_spec(shape, layer):
    nd = len(shape)
    return pl.BlockSpec((None,) + tuple(shape[1:]), lambda i, _l=layer, _nd=nd: (_l,) + (0,) * (_nd - 1),
                        pipeline_mode=pl.Buffered(1))


def _layer_call(x, params, layer, first, final):
    in_hbm = pl.BlockSpec(memory_space=pl.ANY)
    time_major = pl.BlockSpec((ROWS, D_MODEL), lambda i: (i, 0))
    in_specs = [in_hbm if first else time_major] + [
        _layer_spec(p.shape, min(layer, p.shape[0] - 1)) for p in params]
    scratch = [
        pltpu.VMEM((ROWS, D_MODEL), BF16),
        pltpu.VMEM((HALO_ROWS + ROWS, D_BRANCH), F32),
        pltpu.VMEM((S5_BLOCKS, BATCH, 2 * S5_BLOCK_STATE), F32),
        pltpu.VMEM((SGU_HEADS, ROWS, SGU_HEAD_DIM), F32),
        pltpu.VMEM((SGU_HEADS, CHUNK, BATCH * SGU_HEAD_DIM), BF16),
        pltpu.VMEM((SGU_HEADS, ROWS, SGU_HEAD_DIM), F32),
        pltpu.VMEM((ROWS, D_BRANCH), BF16),
        pltpu.VMEM((S5_BLOCKS, S5_STEP * LANES, 2 * S5_BLOCK_STATE), BF16),
        pltpu.VMEM((S5_BLOCKS, 2 * S5_BLOCK_STATE, S5_STEP * LANES), BF16),
        pltpu.VMEM((S5_BLOCKS, S5_STEP * LANES, S5_STEP * LANES), BF16),
    ]
    assert SEQ // T_TILE >= 2 and not (first and final)
    io_buffer = [pltpu.VMEM((2, T_TILE, BATCH, D_MODEL), F32), pltpu.SemaphoreType.DMA((2, BATCH))]
    if first or final:
        scratch += io_buffer
    return pl.pallas_call(
        functools.partial(_layer_kernel, first=first, final=final),
        grid=(SEQ // T_TILE,),
        in_specs=in_specs,
        out_specs=in_hbm if final else time_major,
        out_shape=jax.ShapeDtypeStruct((BATCH, SEQ, D_MODEL) if final else (SEQ * BATCH, D_MODEL), F32),
        scratch_shapes=scratch,
        compiler_params=pltpu.CompilerParams(
            dimension_semantics=("arbitrary",), vmem_limit_bytes=VMEM_LIMIT_BYTES),
        name="trunk_layer_final" if final else "trunk_layer",
    )(x, *params)


def _expansion(n_compact, n_full, compact_of_full):
    e = np.zeros((n_compact, n_full), np.float32)
    e[compact_of_full, np.arange(n_full)] = 1.0
    return e


def _s5_constants():
    gpb = LANES // S5_GROUP
    q, h, p = np.meshgrid(np.arange(2), np.arange(gpb), np.arange(S5_STATE), indexing='ij')
    e_state = _expansion(2 * S5_STATE, 2 * S5_BLOCK_STATE, (q * S5_STATE + p).ravel())
    i, g, c = np.meshgrid(np.arange(S5_STEP), np.arange(gpb), np.arange(S5_GROUP), indexing='ij')
    e_chan = _expansion(S5_STEP * S5_GROUP, S5_STEP * LANES, (i * S5_GROUP + c).ravel())
    same_state = (g.ravel()[:, None] == h.ravel()[None, :]).astype(np.float32)
    same_chan = (g.ravel()[:, None] == g.ravel()[None, :]).astype(np.float32)
    return e_state, e_chan, same_state, same_chan


def _s5_params(lam_re, lam_im, log_dt, b_re, b_im, c_re, c_im):
    dt = jnp.exp(log_dt)[:, None]
    mag = jnp.exp(lam_re * dt)
    ab_re = mag * jnp.cos(lam_im * dt)
    ab_im = mag * jnp.sin(lam_im * dt)
    den = lam_re * lam_re + lam_im * lam_im
    nr = ab_re - 1.0
    ni = ab_im
    k_re = (nr * lam_re + ni * lam_im) / den
    k_im = (ni * lam_re - nr * lam_im) / den
    bb_re = k_re[..., None] * b_re - k_im[..., None] * b_im
    bb_im = k_re[..., None] * b_im + k_im[..., None] * b_re
    gpb = LANES // S5_GROUP

    pw = [(jnp.ones_like(ab_re), jnp.zeros_like(ab_im))]
    for _ in range(S5_STEP):
        p_re, p_im = pw[-1]
        pw.append((p_re * ab_re - p_im * ab_im, p_re * ab_im + p_im * ab_re))
    p_re = jnp.stack([p[0] for p in pw])
    p_im = jnp.stack([p[1] for p in pw])

    anb_re = p_re[..., None] * bb_re - p_im[..., None] * bb_im
    anb_im = p_re[..., None] * bb_im + p_im[..., None] * bb_re
    can_re = c_re * p_re[:, :, None, :] - c_im * p_im[:, :, None, :]
    can_im = c_re * p_im[:, :, None, :] + c_im * p_re[:, :, None, :]

    def by_block(m):
        return m.reshape(m.shape[:2] + (S5_BLOCKS, gpb) + m.shape[3:])

    n_in = jnp.arange(S5_STEP - 1, -1, -1)
    wk_c = by_block(jnp.stack([anb_re[n_in], anb_im[n_in]]))
    wk_c = wk_c.transpose(2, 1, 3, 5, 0, 4).reshape(S5_BLOCKS, S5_STEP * LANES, 2 * S5_STATE)
    n_out = jnp.arange(1, S5_STEP + 1)
    wo_c = by_block(jnp.stack([can_re[n_out], -can_im[n_out]]))
    wo_c = wo_c.transpose(2, 0, 5, 1, 3, 4).reshape(S5_BLOCKS, 2 * S5_STATE, S5_STEP * LANES)
    lags = slice(0, S5_STEP)
    t_n = jnp.sum(can_re[lags, :, None, :, :] * bb_re.transpose(0, 2, 1)[None, :, :, None, :]
                  - can_im[lags, :, None, :, :] * bb_im.transpose(0, 2, 1)[None, :, :, None, :],
                  axis=-1)
    lag = np.arange(S5_STEP)[None, :] - np.arange(S5_STEP)[:, None]
    tz_c = jnp.where((lag >= 0)[:, :, None, None, None], t_n[np.maximum(lag, 0)], 0.0)
    tz_c = by_block(tz_c).transpose(2, 0, 3, 4, 1, 5).reshape(S5_BLOCKS, S5_STEP * LANES, S5_STEP * S5_GROUP)

    def rep(a):
        return jnp.broadcast_to(a.reshape(S5_BLOCKS, 1, S5_BLOCK_STATE), (S5_BLOCKS, SUBLANES, S5_BLOCK_STATE))
    return wk_c.astype(BF16), rep(p_re[S5_STEP]), rep(p_im[S5_STEP]), wo_c.astype(BF16), tz_c.astype(BF16)


def kernel(x, norm_g, w_in, s5_lam_re, s5_lam_im, s5_log_dt, s5_b_re, s5_b_im, s5_c_re, s5_c_im, s5_d, s5_w_glu, s5_b_glu, pool_w, pool_scale, sgu_ln_g, sgu_ln_b, sgu_w, sgu_b, w_branch, w_out, final_norm_g):
    bsz, seq, d = x.shape
    assert (bsz, seq, d) == (BATCH, SEQ, D_MODEL)
    wk, a_re, a_im, wo, tz = jax.vmap(_s5_params)(s5_lam_re, s5_lam_im, s5_log_dt, s5_b_re, s5_b_im,
                                                  s5_c_re, s5_c_im)
    params = (
        norm_g.reshape(DEPTH, 1, d), w_in.astype(BF16), wk, a_re, a_im, wo, tz,
        s5_d.reshape(DEPTH, 1, D_BRANCH), s5_w_glu.astype(BF16), s5_b_glu.reshape(DEPTH, 1, D_BRANCH),
        pool_w.astype(BF16), pool_scale.reshape(DEPTH, 1, D_BRANCH),
        sgu_ln_g.reshape(DEPTH, 1, D_BRANCH), sgu_ln_b.reshape(DEPTH, 1, D_BRANCH),
        sgu_w, sgu_b.reshape(DEPTH, SGU_HEADS, SGU_BLOCK, 1),
        w_branch.astype(BF16), w_out.astype(BF16), final_norm_g.reshape(1, 1, d),
    )
    e_state, e_chan, same_state, same_chan = _s5_constants()
    params += tuple(jnp.asarray(c[None], BF16) for c in (
        e_state, e_state.T, e_chan, same_state, same_state.T, same_chan))
    for l in range(DEPTH):
        x = _layer_call(x, params, l, first=(l == 0), final=(l == DEPTH - 1))
    return x
```

```python
import functools

import jax
import jax.numpy as jnp
import numpy as np
from jax import lax
from jax.experimental import pallas as pl
from jax.experimental.pallas import tpu as pltpu

D_MODEL = 1024
BATCH = 16
SEQ = 2048
DEPTH = 2
CHUNK = 64
D_BRANCH = 512
N_BRANCH = 3
S5_GROUP = 16
S5_GROUPS = D_BRANCH // S5_GROUP
S5_STATE = 64
POOL_WINDOWS = (2, 4, 8, 16)
POOL_GROUP = D_BRANCH // len(POOL_WINDOWS)
SGU_BLOCK = 128
SGU_HEADS = 4
SGU_HEAD_DIM = D_BRANCH // SGU_HEADS
RMS_EPS = 1e-6
LN_EPS = 1e-5

LANES = 128
SUBLANES = 8
S5_BLOCKS = D_BRANCH // LANES
S5_BLOCK_STATE = (LANES // S5_GROUP) * S5_STATE
T_TILE = CHUNK
ROWS = T_TILE * BATCH
ROW_CHUNK = 512
N_CHUNKS = ROWS // ROW_CHUNK
T_CHUNK = ROW_CHUNK // BATCH
HALO_ROWS = max(POOL_WINDOWS) * BATCH
N_SUB = BATCH // SUBLANES
S5_STEP = 2
VMEM_LIMIT_BYTES = 62 * 1024 * 1024

OFF_A_VAL, OFF_A_GATE, OFF_B_VAL, OFF_B_GATE, OFF_C_U, OFF_C_V, OFF_C_GATE, OFF_GATES = (
    0, 512, 1024, 1536, 2048, 2560, 3072, 3584)

F32 = jnp.float32
BF16 = jnp.bfloat16


def _rows(c, n=ROW_CHUNK, base=0):
    if isinstance(c, int):
        return pl.ds(base + c * n, n)
    return pl.ds(pl.multiple_of(base + c * n, SUBLANES), n)


def _z_row(c, t, bh):
    tb, tl = divmod(t, SUBLANES)
    static = (tb * BATCH + bh * SUBLANES) * SUBLANES + tl
    return c * (T_CHUNK * BATCH) + static


def _dot(a, b):
    return jnp.dot(a, b, preferred_element_type=F32)


def _for_chunks(body):
    for c in range(N_CHUNKS):
        body(c)


def _tile_copies(hbm_ref, buf_ref, sem_ref, tile, slot, to_hbm):
    copies = []
    for b in range(BATCH):
        hbm = hbm_ref.at[b, pl.ds(tile * T_TILE, T_TILE), :]
        vmem = buf_ref.at[slot, :, b, :]
        src, dst = (vmem, hbm) if to_hbm else (hbm, vmem)
        copies.append(pltpu.make_async_copy(src, dst, sem_ref.at[slot, b]))
    return copies


def _layer_kernel(x_ref, g_ref, win_ref, wk_ref, are_ref, aim_ref, wo_ref, tz_ref, dsk_ref, wglu_ref, bglu_ref,
                  pw_ref, psc_ref, lng_ref, lnb_ref, sw_ref, sb_ref, wbr_ref, wout_ref, fg_ref,
                  o_ref,
                  h_scr, ext_scr, st_scr, vn_scr, vsave_scr, z_scr, ya_scr, *io_scr,
                  first, final):
    i = pl.program_id(0)
    n_tiles = pl.num_programs(0)
    slot = i % 2
    half = S5_BLOCK_STATE
    if first:
        xbuf, xsem = io_scr
    if final:
        obuf, osem = io_scr

    @pl.when(i == 0)
    def _():
        st_scr[...] = jnp.zeros_like(st_scr)
        ext_scr[0:HALO_ROWS, :] = jnp.zeros((HALO_ROWS, D_BRANCH), F32)
        vsave_scr[...] = jnp.zeros_like(vsave_scr)

    if first:
        @pl.when(i == 0)
        def _():
            for cp in _tile_copies(x_ref, xbuf, xsem, 0, 0, to_hbm=False):
                cp.start()

        @pl.when(i + 1 < n_tiles)
        def _():
            for cp in _tile_copies(x_ref, xbuf, xsem, i + 1, 1 - slot, to_hbm=False):
                cp.start()

        for cp in _tile_copies(x_ref, xbuf, xsem, i, slot, to_hbm=False):
            cp.wait()

    if final:
        @pl.when(i >= 2)
        def _():
            for cp in _tile_copies(o_ref, obuf, osem, i - 2, slot, to_hbm=True):
                cp.wait()

    def x_rows(c):
        if first:
            return xbuf[slot, pl.ds(c * T_CHUNK, T_CHUNK), :, :].reshape(ROW_CHUNK, D_MODEL)
        return x_ref[_rows(c), :]

    def loop_a(c):
        r = _rows(c)
        xc = x_rows(c)
        ms = jnp.mean(xc * xc, axis=-1, keepdims=True)
        hc = (xc * lax.rsqrt(ms + RMS_EPS) * g_ref[...]).astype(BF16)
        h_scr[r, :] = hc

        def project_pool_value():
            ext_scr[_rows(c, base=HALO_ROWS), :] = _dot(hc, win_ref[:, OFF_B_VAL:OFF_B_VAL + D_BRANCH])

        def project_sgu_value():
            v = _dot(hc, win_ref[:, OFF_C_V:OFF_C_V + D_BRANCH])
            mu = jnp.mean(v, axis=-1, keepdims=True)
            vc = v - mu
            var = jnp.mean(vc * vc, axis=-1, keepdims=True)
            vn = vc * lax.rsqrt(var + LN_EPS) * lng_ref[...] + lnb_ref[...]
            for hd in range(SGU_HEADS):
                for bh in range(N_SUB):
                    slab = jnp.concatenate(
                        [vn[t * BATCH + bh * SUBLANES:t * BATCH + (bh + 1) * SUBLANES,
                            hd * SGU_HEAD_DIM:(hd + 1) * SGU_HEAD_DIM] for t in range(T_CHUNK)], axis=0)
                    vn_scr[hd, _rows(c, n=T_CHUNK * SUBLANES, base=bh * T_TILE * SUBLANES), :] = slab

        gate = []

        def project_s5_gate():
            gate.append(_dot(hc, win_ref[:, OFF_A_GATE:OFF_A_GATE + D_BRANCH]))

        independent = [project_pool_value, project_sgu_value, project_s5_gate]

        aval = _dot(hc, win_ref[:, OFF_A_VAL:OFF_A_VAL + D_BRANCH])
        n_blk = T_CHUNK // S5_STEP
        ys = []
        for j in range(S5_BLOCKS):
            u = aval[:, j * LANES:(j + 1) * LANES].astype(BF16)
            ub = jnp.concatenate(
                [jnp.concatenate([u[(S5_STEP * blk + i) * BATCH:(S5_STEP * blk + i + 1) * BATCH, :]
                                  for blk in range(n_blk)], axis=0) for i in range(S5_STEP)], axis=1)
            cin = _dot(ub, wk_ref[j])
            if independent:
                independent.pop(0)()
            a_re, a_im = are_ref[j], aim_ref[j]
            state = []
            for s in range(N_SUB):
                rs = slice(SUBLANES * s, SUBLANES * (s + 1))
                state += [st_scr[j, rs, 0:half], st_scr[j, rs, half:2 * half]]
            hp_rows = []
            for blk in range(n_blk):
                for s in range(N_SUB):
                    h_re, h_im = state[2 * s], state[2 * s + 1]
                    hp_rows.append(jnp.concatenate([h_re, h_im], axis=1))
                    b = cin[blk * BATCH + s * SUBLANES:blk * BATCH + (s + 1) * SUBLANES, :]
                    state[2 * s] = a_re * h_re - a_im * h_im + b[:, :half]
                    state[2 * s + 1] = a_re * h_im + a_im * h_re + b[:, half:]
            for s in range(N_SUB):
                rs = slice(SUBLANES * s, SUBLANES * (s + 1))
                st_scr[j, rs, 0:half] = state[2 * s]
                st_scr[j, rs, half:2 * half] = state[2 * s + 1]
            hp = jnp.concatenate(hp_rows, axis=0).astype(BF16)
            yb = _dot(hp, wo_ref[j]) + _dot(ub, tz_ref[j])
            ys.append(jnp.concatenate(
                [yb[blk * BATCH:(blk + 1) * BATCH, i * LANES:(i + 1) * LANES]
                 for blk in range(n_blk) for i in range(S5_STEP)], axis=0))
        while independent:
            independent.pop(0)()
        y = jnp.concatenate(ys, axis=1) + dsk_ref[...] * aval
        y = jax.nn.gelu(y)
        y = y * jax.nn.sigmoid(_dot(y.astype(BF16), wglu_ref[...]) + bglu_ref[...])
        ya_scr[r, :] = (y * jax.nn.silu(gate[0])).astype(BF16)
    _for_chunks(loop_a)

    second = (i % 2) == 1
    t0 = pl.multiple_of((i % 2) * CHUNK, CHUNK)
    for hd in range(SGU_HEADS):
        w_rows = sw_ref[hd, pl.ds(t0, CHUNK), :]
        w_lo, w_hi = w_rows[:, 0:CHUNK], w_rows[:, CHUNK:2 * CHUNK]
        w_cur = jnp.where(second, w_hi, w_lo).astype(BF16)
        w_prev = jnp.where(second, w_lo, 0.0).astype(BF16)
        v_cur = jnp.concatenate(
            [vn_scr[hd, pl.ds((b // SUBLANES) * T_TILE * SUBLANES + b % SUBLANES, CHUNK, stride=SUBLANES), :]
             for b in range(BATCH)], axis=1).astype(BF16)
        zt = _dot(w_cur, v_cur) + _dot(w_prev, vsave_scr[hd]) + sb_ref[hd, pl.ds(t0, CHUNK), :]
        vsave_scr[hd] = v_cur
        for tb in range(CHUNK // SUBLANES):
            z_scr[hd, tb * BATCH * SUBLANES:(tb + 1) * BATCH * SUBLANES, :] = jnp.concatenate(
                [zt[tb * SUBLANES:(tb + 1) * SUBLANES, b * SGU_HEAD_DIM:(b + 1) * SGU_HEAD_DIM]
                 for b in range(BATCH)], axis=0)

    def loop_b(c):
        r = _rows(c)
        hc = h_scr[r, :]

        t_idx = lax.broadcasted_iota(jnp.int32, (ROW_CHUNK, 1), 0) // BATCH
        pos = (i * T_TILE + c * T_CHUNK + 1 + t_idx).astype(F32)
        ys = []
        for gi, w in enumerate(POOL_WINDOWS):
            cols = slice(gi * POOL_GROUP, (gi + 1) * POOL_GROUP)
            cur = ext_scr[_rows(c, base=HALO_ROWS), cols]
            acc = cur
            for k in range(1, w):
                acc = acc + ext_scr[_rows(c, base=HALO_ROWS - k * BATCH), cols]
            p = acc / jnp.minimum(pos, float(w)) - cur
            ys.append(_dot(p.astype(BF16), pw_ref[gi]))
        y = jnp.concatenate(ys, axis=1) * psc_ref[...]
        gate = _dot(hc, win_ref[:, OFF_B_GATE:OFF_B_GATE + D_BRANCH])
        yb = (y * jax.nn.silu(gate)).astype(BF16)

        z = jnp.concatenate(
            [jnp.concatenate(
                [z_scr[hd, pl.ds(_z_row(c, t, bh), SUBLANES, stride=SUBLANES), :]
                 for t in range(T_CHUNK) for bh in range(N_SUB)], axis=0)
             for hd in range(SGU_HEADS)], axis=1)
        u = _dot(hc, win_ref[:, OFF_C_U:OFF_C_U + D_BRANCH])
        gate = _dot(hc, win_ref[:, OFF_C_GATE:OFF_C_GATE + D_BRANCH])
        yc = (u * z * jax.nn.silu(gate)).astype(BF16)

        merged = None
        for k, yk in enumerate((ya_scr[r, :], yb, yc)):
            mg = _dot(hc, win_ref[:, OFF_GATES + k * D_MODEL:OFF_GATES + (k + 1) * D_MODEL])
            term = jax.nn.sigmoid(mg) * _dot(yk, wbr_ref[k])
            merged = term if merged is None else merged + term
        out = x_rows(c) + _dot(merged.astype(BF16), wout_ref[...])
        if final:
            ms = jnp.mean(out * out, axis=-1, keepdims=True)
            out = out * lax.rsqrt(ms + RMS_EPS) * fg_ref[...]
            obuf[slot, pl.ds(c * T_CHUNK, T_CHUNK), :, :] = out.reshape(T_CHUNK, BATCH, D_MODEL)
        else:
            o_ref[r, :] = out
    _for_chunks(loop_b)

    ext_scr[0:HALO_ROWS, :] = ext_scr[ROWS:ROWS + HALO_ROWS, :]

    if final:
        for cp in _tile_copies(o_ref, obuf, osem, i, slot, to_hbm=True):
            cp.start()

        @pl.when(i == n_tiles - 1)
        def _():
            for cp in _tile_copies(o_ref, obuf, osem, i - 1, 1 - slot, to_hbm=True):
                cp.wait()
            for cp in _tile_copies(o_ref, obuf, osem, i, slot, to_hbm=True):
                cp.wait()


def _layer_spec(shape, layer):
    nd = len(shape)
    return pl.BlockSpec((None,) + tuple(shape[1:]), lambda i, _l=layer, _nd=nd: (_l,) + (0,) * (_nd - 1),
                        pipeline_mode=pl.Buffered(1))


def _layer_call(x, params, layer, first, final):
    in_hbm = pl.BlockSpec(memory_space=pl.ANY)
    time_major = pl.BlockSpec((ROWS, D_MODEL), lambda i: (i, 0))
    in_specs = [in_hbm if first else time_major] + [
        _layer_spec(p.shape, min(layer, p.shape[0] - 1)) for p in params]
    scratch = [
        pltpu.VMEM((ROWS, D_MODEL), BF16),
        pltpu.VMEM((HALO_ROWS + ROWS, D_BRANCH), F32),
        pltpu.VMEM((S5_BLOCKS, BATCH, 2 * S5_BLOCK_STATE), F32),
        pltpu.VMEM((SGU_HEADS, ROWS, SGU_HEAD_DIM), F32),
        pltpu.VMEM((SGU_HEADS, CHUNK, BATCH * SGU_HEAD_DIM), BF16),
        pltpu.VMEM((SGU_HEADS, ROWS, SGU_HEAD_DIM), F32),
        pltpu.VMEM((ROWS, D_BRANCH), BF16),
    ]
    assert SEQ // T_TILE >= 2 and not (first and final)
    io_buffer = [pltpu.VMEM((2, T_TILE, BATCH, D_MODEL), F32), pltpu.SemaphoreType.DMA((2, BATCH))]
    if first or final:
        scratch += io_buffer
    return pl.pallas_call(
        functools.partial(_layer_kernel, first=first, final=final),
        grid=(SEQ // T_TILE,),
        in_specs=in_specs,
        out_specs=in_hbm if final else time_major,
        out_shape=jax.ShapeDtypeStruct((BATCH, SEQ, D_MODEL) if final else (SEQ * BATCH, D_MODEL), F32),
        scratch_shapes=scratch,
        compiler_params=pltpu.CompilerParams(
            dimension_semantics=("arbitrary",), vmem_limit_bytes=VMEM_LIMIT_BYTES),
        name="trunk_layer_final" if final else "trunk_layer",
    )(x, *params)


def _expansion(n_compact, n_full, compact_of_full):
    e = np.zeros((n_compact, n_full), np.float32)
    e[compact_of_full, np.arange(n_full)] = 1.0
    return e


def _s5_constants():
    gpb = LANES // S5_GROUP
    q, h, p = np.meshgrid(np.arange(2), np.arange(gpb), np.arange(S5_STATE), indexing='ij')
    e_state = _expansion(2 * S5_STATE, 2 * S5_BLOCK_STATE, (q * S5_STATE + p).ravel())
    i, g, c = np.meshgrid(np.arange(S5_STEP), np.arange(gpb), np.arange(S5_GROUP), indexing='ij')
    e_chan = _expansion(S5_STEP * S5_GROUP, S5_STEP * LANES, (i * S5_GROUP + c).ravel())
    same_state = (g.ravel()[:, None] == h.ravel()[None, :]).astype(np.float32)
    same_chan = (g.ravel()[:, None] == g.ravel()[None, :]).astype(np.float32)
    return e_state, e_chan, same_state, same_chan


def _s5_params(lam_re, lam_im, log_dt, b_re, b_im, c_re, c_im):
    hi = lax.Precision.HIGHEST
    dt = jnp.exp(log_dt)[:, None]
    mag = jnp.exp(lam_re * dt)
    ab_re = mag * jnp.cos(lam_im * dt)
    ab_im = mag * jnp.sin(lam_im * dt)
    den = lam_re * lam_re + lam_im * lam_im
    nr = ab_re - 1.0
    ni = ab_im
    k_re = (nr * lam_re + ni * lam_im) / den
    k_im = (ni * lam_re - nr * lam_im) / den
    bb_re = k_re[..., None] * b_re - k_im[..., None] * b_im
    bb_im = k_re[..., None] * b_im + k_im[..., None] * b_re
    gpb = LANES // S5_GROUP

    pw = [(jnp.ones_like(ab_re), jnp.zeros_like(ab_im))]
    for _ in range(S5_STEP):
        p_re, p_im = pw[-1]
        pw.append((p_re * ab_re - p_im * ab_im, p_re * ab_im + p_im * ab_re))
    p_re = jnp.stack([p[0] for p in pw])
    p_im = jnp.stack([p[1] for p in pw])

    anb_re = p_re[..., None] * bb_re - p_im[..., None] * bb_im
    anb_im = p_re[..., None] * bb_im + p_im[..., None] * bb_re
    can_re = c_re * p_re[:, :, None, :] - c_im * p_im[:, :, None, :]
    can_im = c_re * p_im[:, :, None, :] + c_im * p_re[:, :, None, :]

    e_state, e_chan, same_state, same_chan = _s5_constants()

    def by_block(m):
        return m.reshape(m.shape[:2] + (S5_BLOCKS, gpb) + m.shape[3:])

    n_in = jnp.arange(S5_STEP - 1, -1, -1)
    wk_c = by_block(jnp.stack([anb_re[n_in], anb_im[n_in]]))
    wk_c = wk_c.transpose(2, 1, 3, 5, 0, 4).reshape(S5_BLOCKS, S5_STEP * LANES, 2 * S5_STATE)
    wk = jnp.einsum('jrs,st->jrt', wk_c, e_state, precision=hi) * same_state
    n_out = jnp.arange(1, S5_STEP + 1)
    wo_c = by_block(jnp.stack([can_re[n_out], -can_im[n_out]]))
    wo_c = wo_c.transpose(2, 0, 5, 1, 3, 4).reshape(S5_BLOCKS, 2 * S5_STATE, S5_STEP * LANES)
    wo = jnp.einsum('ts,jsr->jtr', e_state.T, wo_c, precision=hi) * same_state.T
    lags = slice(0, S5_STEP)
    t_n = jnp.sum(can_re[lags, :, None, :, :] * bb_re.transpose(0, 2, 1)[None, :, :, None, :]
                  - can_im[lags, :, None, :, :] * bb_im.transpose(0, 2, 1)[None, :, :, None, :],
                  axis=-1)
    lag = np.arange(S5_STEP)[None, :] - np.arange(S5_STEP)[:, None]
    tz_c = jnp.where((lag >= 0)[:, :, None, None, None], t_n[np.maximum(lag, 0)], 0.0)
    tz_c = by_block(tz_c).transpose(2, 0, 3, 4, 1, 5).reshape(S5_BLOCKS, S5_STEP * LANES, S5_STEP * S5_GROUP)
    tz = jnp.einsum('jrs,st->jrt', tz_c, e_chan, precision=hi) * same_chan

    def rep(a):
        return jnp.broadcast_to(a.reshape(S5_BLOCKS, 1, S5_BLOCK_STATE), (S5_BLOCKS, SUBLANES, S5_BLOCK_STATE))
    return wk.astype(BF16), rep(p_re[S5_STEP]), rep(p_im[S5_STEP]), wo.astype(BF16), tz.astype(BF16)


def kernel(x, norm_g, w_in, s5_lam_re, s5_lam_im, s5_log_dt, s5_b_re, s5_b_im, s5_c_re, s5_c_im, s5_d, s5_w_glu, s5_b_glu, pool_w, pool_scale, sgu_ln_g, sgu_ln_b, sgu_w, sgu_b, w_branch, w_out, final_norm_g):
    bsz, seq, d = x.shape
    assert (bsz, seq, d) == (BATCH, SEQ, D_MODEL)
    wk, a_re, a_im, wo, tz = jax.vmap(_s5_params)(s5_lam_re, s5_lam_im, s5_log_dt, s5_b_re, s5_b_im,
                                                  s5_c_re, s5_c_im)
    params = (
        norm_g.reshape(DEPTH, 1, d), w_in.astype(BF16), wk, a_re, a_im, wo, tz,
        s5_d.reshape(DEPTH, 1, D_BRANCH), s5_w_glu.astype(BF16), s5_b_glu.reshape(DEPTH, 1, D_BRANCH),
        pool_w.astype(BF16), pool_scale.reshape(DEPTH, 1, D_BRANCH),
        sgu_ln_g.reshape(DEPTH, 1, D_BRANCH), sgu_ln_b.reshape(DEPTH, 1, D_BRANCH),
        sgu_w, sgu_b.reshape(DEPTH, SGU_HEADS, SGU_BLOCK, 1),
        w_branch.astype(BF16), w_out.astype(BF16), final_norm_g.reshape(1, 1, d),
    )
    for l in range(DEPTH):
        x = _layer_call(x, params, l, first=(l == 0), final=(l == DEPTH - 1))
    return x
```

```python
import functools

import jax
import jax.numpy as jnp
import numpy as np
from jax import lax
from jax.experimental import pallas as pl
from jax.experimental.pallas import tpu as pltpu

D_MODEL = 1024
BATCH = 16
SEQ = 2048
DEPTH = 2
CHUNK = 64
D_BRANCH = 512
N_BRANCH = 3
S5_GROUP = 16
S5_GROUPS = D_BRANCH // S5_GROUP
S5_STATE = 64
POOL_WINDOWS = (2, 4, 8, 16)
POOL_GROUP = D_BRANCH // len(POOL_WINDOWS)
SGU_BLOCK = 128
SGU_HEADS = 4
SGU_HEAD_DIM = D_BRANCH // SGU_HEADS
RMS_EPS = 1e-6
LN_EPS = 1e-5

LANES = 128
SUBLANES = 8
S5_BLOCKS = D_BRANCH // LANES
S5_BLOCK_STATE = (LANES // S5_GROUP) * S5_STATE
T_TILE = CHUNK
ROWS = T_TILE * BATCH
ROW_CHUNK = 512
N_CHUNKS = ROWS // ROW_CHUNK
T_CHUNK = ROW_CHUNK // BATCH
HALO_ROWS = max(POOL_WINDOWS) * BATCH
N_SUB = BATCH // SUBLANES
S5_STEP = 2
VMEM_LIMIT_BYTES = 62 * 1024 * 1024

OFF_A_VAL, OFF_A_GATE, OFF_B_VAL, OFF_B_GATE, OFF_C_U, OFF_C_V, OFF_C_GATE, OFF_GATES = (
    0, 512, 1024, 1536, 2048, 2560, 3072, 3584)

F32 = jnp.float32
BF16 = jnp.bfloat16


def _rows(c, n=ROW_CHUNK, base=0):
    if isinstance(c, int):
        return pl.ds(base + c * n, n)
    return pl.ds(pl.multiple_of(base + c * n, SUBLANES), n)


def _z_row(c, t, bh):
    tb, tl = divmod(t, SUBLANES)
    static = (tb * BATCH + bh * SUBLANES) * SUBLANES + tl
    return c * (T_CHUNK * BATCH) + static


def _dot(a, b):
    return jnp.dot(a, b, preferred_element_type=F32)


def _for_chunks(body):
    for c in range(N_CHUNKS):
        body(c)


def _tile_copies(hbm_ref, buf_ref, sem_ref, tile, slot, to_hbm):
    copies = []
    for b in range(BATCH):
        hbm = hbm_ref.at[b, pl.ds(tile * T_TILE, T_TILE), :]
        vmem = buf_ref.at[slot, :, b, :]
        src, dst = (vmem, hbm) if to_hbm else (hbm, vmem)
        copies.append(pltpu.make_async_copy(src, dst, sem_ref.at[slot, b]))
    return copies


def _layer_kernel(x_ref, g_ref, win_ref, wkc_ref, are_ref, aim_ref, woc_ref, tzc_ref, dsk_ref, wglu_ref, bglu_ref,
                  pw_ref, psc_ref, lng_ref, lnb_ref, sw_ref, sb_ref, wbr_ref, wout_ref, fg_ref,
                  es_ref, est_ref, ec_ref, ms_ref, mst_ref, mc_ref,
                  o_ref,
                  h_scr, ext_scr, st_scr, vn_scr, vsave_scr, z_scr, ya_scr, wk_ref, wo_ref, tz_ref, *io_scr,
                  first, final):
    i = pl.program_id(0)
    n_tiles = pl.num_programs(0)
    slot = i % 2
    half = S5_BLOCK_STATE
    if first:
        xbuf, xsem = io_scr
    if final:
        obuf, osem = io_scr

    @pl.when(i == 0)
    def _():
        st_scr[...] = jnp.zeros_like(st_scr)
        ext_scr[0:HALO_ROWS, :] = jnp.zeros((HALO_ROWS, D_BRANCH), F32)
        vsave_scr[...] = jnp.zeros_like(vsave_scr)
        for j in range(S5_BLOCKS):
            wk_ref[j] = (_dot(wkc_ref[j], es_ref[...]) * ms_ref[...]).astype(BF16)
            wo_ref[j] = (_dot(est_ref[...], woc_ref[j]) * mst_ref[...]).astype(BF16)
            tz_ref[j] = (_dot(tzc_ref[j], ec_ref[...]) * mc_ref[...]).astype(BF16)

    if first:
        @pl.when(i == 0)
        def _():
            for cp in _tile_copies(x_ref, xbuf, xsem, 0, 0, to_hbm=False):
                cp.start()

        @pl.when(i + 1 < n_tiles)
        def _():
            for cp in _tile_copies(x_ref, xbuf, xsem, i + 1, 1 - slot, to_hbm=False):
                cp.start()

        for cp in _tile_copies(x_ref, xbuf, xsem, i, slot, to_hbm=False):
            cp.wait()

    if final:
        @pl.when(i >= 2)
        def _():
            for cp in _tile_copies(o_ref, obuf, osem, i - 2, slot, to_hbm=True):
                cp.wait()

    def x_rows(c):
        if first:
            return xbuf[slot, pl.ds(c * T_CHUNK, T_CHUNK), :, :].reshape(ROW_CHUNK, D_MODEL)
        return x_ref[_rows(c), :]

    def project(r, lo, width):
        return _dot(h_scr[r, :], win_ref[:, lo:lo + width])

    def loop_a(c):
        r = _rows(c)
        xc = x_rows(c)
        ms = jnp.mean(xc * xc, axis=-1, keepdims=True)
        h_scr[r, :] = (xc * lax.rsqrt(ms + RMS_EPS) * g_ref[...]).astype(BF16)

        def project_pool_value():
            ext_scr[_rows(c, base=HALO_ROWS), :] = project(r, OFF_B_VAL, D_BRANCH)

        def project_sgu_value():
            v = project(r, OFF_C_V, D_BRANCH)
            mu = jnp.mean(v, axis=-1, keepdims=True)
            vc = v - mu
            var = jnp.mean(vc * vc, axis=-1, keepdims=True)
            vn = vc * lax.rsqrt(var + LN_EPS) * lng_ref[...] + lnb_ref[...]
            for hd in range(SGU_HEADS):
                for bh in range(N_SUB):
                    slab = jnp.concatenate(
                        [vn[t * BATCH + bh * SUBLANES:t * BATCH + (bh + 1) * SUBLANES,
                            hd * SGU_HEAD_DIM:(hd + 1) * SGU_HEAD_DIM] for t in range(T_CHUNK)], axis=0)
                    vn_scr[hd, _rows(c, n=T_CHUNK * SUBLANES, base=bh * T_TILE * SUBLANES), :] = slab

        gate = []

        def project_s5_gate():
            gate.append(project(r, OFF_A_GATE, D_BRANCH))

        independent = [project_pool_value, project_sgu_value, project_s5_gate]

        aval = project(r, OFF_A_VAL, D_BRANCH)
        n_blk = T_CHUNK // S5_STEP
        ys = []
        for j in range(S5_BLOCKS):
            u = aval[:, j * LANES:(j + 1) * LANES].astype(BF16)
            ub = jnp.concatenate(
                [jnp.concatenate([u[(S5_STEP * blk + i) * BATCH:(S5_STEP * blk + i + 1) * BATCH, :]
                                  for blk in range(n_blk)], axis=0) for i in range(S5_STEP)], axis=1)
            cin = _dot(ub, wk_ref[j])
            if independent:
                independent.pop(0)()
            a_re, a_im = are_ref[j], aim_ref[j]
            state = []
            for s in range(N_SUB):
                rs = slice(SUBLANES * s, SUBLANES * (s + 1))
                state += [st_scr[j, rs, 0:half], st_scr[j, rs, half:2 * half]]
            hp_rows = []
            for blk in range(n_blk):
                for s in range(N_SUB):
                    h_re, h_im = state[2 * s], state[2 * s + 1]
                    hp_rows.append(jnp.concatenate([h_re, h_im], axis=1))
                    b = cin[blk * BATCH + s * SUBLANES:blk * BATCH + (s + 1) * SUBLANES, :]
                    state[2 * s] = a_re * h_re - a_im * h_im + b[:, :half]
                    state[2 * s + 1] = a_re * h_im + a_im * h_re + b[:, half:]
            for s in range(N_SUB):
                rs = slice(SUBLANES * s, SUBLANES * (s + 1))
                st_scr[j, rs, 0:half] = state[2 * s]
                st_scr[j, rs, half:2 * half] = state[2 * s + 1]
            hp = jnp.concatenate(hp_rows, axis=0).astype(BF16)
            yb = _dot(hp, wo_ref[j]) + _dot(ub, tz_ref[j])
            ys.append(jnp.concatenate(
                [yb[blk * BATCH:(blk + 1) * BATCH, i * LANES:(i + 1) * LANES]
                 for blk in range(n_blk) for i in range(S5_STEP)], axis=0))
        while independent:
            independent.pop(0)()
        y = jnp.concatenate(ys, axis=1) + dsk_ref[...] * aval
        y = jax.nn.gelu(y)
        y = y * jax.nn.sigmoid(_dot(y.astype(BF16), wglu_ref[...]) + bglu_ref[...])
        ya_scr[r, :] = (y * jax.nn.silu(gate[0])).astype(BF16)
    _for_chunks(loop_a)

    second = (i % 2) == 1
    t0 = pl.multiple_of((i % 2) * CHUNK, CHUNK)
    for hd in range(SGU_HEADS):
        w_rows = sw_ref[hd, pl.ds(t0, CHUNK), :]
        w_lo, w_hi = w_rows[:, 0:CHUNK], w_rows[:, CHUNK:2 * CHUNK]
        w_cur = jnp.where(second, w_hi, w_lo).astype(BF16)
        w_prev = jnp.where(second, w_lo, 0.0).astype(BF16)
        v_cur = jnp.concatenate(
            [vn_scr[hd, pl.ds((b // SUBLANES) * T_TILE * SUBLANES + b % SUBLANES, CHUNK, stride=SUBLANES), :]
             for b in range(BATCH)], axis=1).astype(BF16)
        zt = _dot(w_cur, v_cur) + _dot(w_prev, vsave_scr[hd]) + sb_ref[hd, pl.ds(t0, CHUNK), :]
        vsave_scr[hd] = v_cur
        for tb in range(CHUNK // SUBLANES):
            z_scr[hd, tb * BATCH * SUBLANES:(tb + 1) * BATCH * SUBLANES, :] = jnp.concatenate(
                [zt[tb * SUBLANES:(tb + 1) * SUBLANES, b * SGU_HEAD_DIM:(b + 1) * SGU_HEAD_DIM]
                 for b in range(BATCH)], axis=0)

    def loop_b(c):
        r = _rows(c)

        t_idx = lax.broadcasted_iota(jnp.int32, (ROW_CHUNK, 1), 0) // BATCH
        pos = (i * T_TILE + c * T_CHUNK + 1 + t_idx).astype(F32)
        ys = []
        for gi, w in enumerate(POOL_WINDOWS):
            cols = slice(gi * POOL_GROUP, (gi + 1) * POOL_GROUP)
            cur = ext_scr[_rows(c, base=HALO_ROWS), cols]
            acc = cur
            for k in range(1, w):
                acc = acc + ext_scr[_rows(c, base=HALO_ROWS - k * BATCH), cols]
            p = acc / jnp.minimum(pos, float(w)) - cur
            ys.append(_dot(p.astype(BF16), pw_ref[gi]))
        y = jnp.concatenate(ys, axis=1) * psc_ref[...]
        gate = project(r, OFF_B_GATE, D_BRANCH)
        yb = (y * jax.nn.silu(gate)).astype(BF16)

        z = jnp.concatenate(
            [jnp.concatenate(
                [z_scr[hd, pl.ds(_z_row(c, t, bh), SUBLANES, stride=SUBLANES), :]
                 for t in range(T_CHUNK) for bh in range(N_SUB)], axis=0)
             for hd in range(SGU_HEADS)], axis=1)
        u = project(r, OFF_C_U, D_BRANCH)
        gate = project(r, OFF_C_GATE, D_BRANCH)
        yc = (u * z * jax.nn.silu(gate)).astype(BF16)

        merged = None
        for k, yk in enumerate((ya_scr[r, :], yb, yc)):
            mg = project(r, OFF_GATES + k * D_MODEL, D_MODEL)
            term = jax.nn.sigmoid(mg) * _dot(yk, wbr_ref[k])
            merged = term if merged is None else merged + term
        out = x_rows(c) + _dot(merged.astype(BF16), wout_ref[...])
        if final:
            ms = jnp.mean(out * out, axis=-1, keepdims=True)
            out = out * lax.rsqrt(ms + RMS_EPS) * fg_ref[...]
            obuf[slot, pl.ds(c * T_CHUNK, T_CHUNK), :, :] = out.reshape(T_CHUNK, BATCH, D_MODEL)
        else:
            o_ref[r, :] = out
    _for_chunks(loop_b)

    ext_scr[0:HALO_ROWS, :] = ext_scr[ROWS:ROWS + HALO_ROWS, :]

    if final:
        for cp in _tile_copies(o_ref, obuf, osem, i, slot, to_hbm=True):
            cp.start()

        @pl.when(i == n_tiles - 1)
        def _():
            for cp in _tile_copies(o_ref, obuf, osem, i - 1, 1 - slot, to_hbm=True):
                cp.wait()
            for cp in _tile_copies(o_ref, obuf, osem, i, slot, to_hbm=True):
                cp.wait()


def _layer_spec(shape, layer):
    nd = len(shape)
    return pl.BlockSpec((None,) + tuple(shape[1:]), lambda i, _l=layer, _nd=nd: (_l,) + (0,) * (_nd - 1),
                        pipeline_mode=pl.Buffered(1))


def _layer_call(x, params, layer, first, final):
    in_hbm = pl.BlockSpec(memory_space=pl.ANY)
    time_major = pl.BlockSpec((ROWS, D_MODEL), lambda i: (i, 0))
    in_specs = [in_hbm if first else time_major] + [
        _layer_spec(p.shape, min(layer, p.shape[0] - 1)) for p in params]
    scratch = [
        pltpu.VMEM((ROWS, D_MODEL), BF16),
        pltpu.VMEM((HALO_ROWS + ROWS, D_BRANCH), F32),
        pltpu.VMEM((S5_BLOCKS, BATCH, 2 * S5_BLOCK_STATE), F32),
        pltpu.VMEM((SGU_HEADS, ROWS, SGU_HEAD_DIM), F32),
        pltpu.VMEM((SGU_HEADS, CHUNK, BATCH * SGU_HEAD_DIM), BF16),
        pltpu.VMEM((SGU_HEADS, ROWS, SGU_HEAD_DIM), F32),
        pltpu.VMEM((ROWS, D_BRANCH), BF16),
        pltpu.VMEM((S5_BLOCKS, S5_STEP * LANES, 2 * S5_BLOCK_STATE), BF16),
        pltpu.VMEM((S5_BLOCKS, 2 * S5_BLOCK_STATE, S5_STEP * LANES), BF16),
        pltpu.VMEM((S5_BLOCKS, S5_STEP * LANES, S5_STEP * LANES), BF16),
    ]
    assert SEQ // T_TILE >= 2 and not (first and final)
    io_buffer = [pltpu.VMEM((2, T_TILE, BATCH, D_MODEL), F32), pltpu.SemaphoreType.DMA((2, BATCH))]
    if first or final:
        scratch += io_buffer
    return pl.pallas_call(
        functools.partial(_layer_kernel, first=first, final=final),
        grid=(SEQ // T_TILE,),
        in_specs=in_specs,
        out_specs=in_hbm if final else time_major,
        out_shape=jax.ShapeDtypeStruct((BATCH, SEQ, D_MODEL) if final else (SEQ * BATCH, D_MODEL), F32),
        scratch_shapes=scratch,
        compiler_params=pltpu.CompilerParams(
            dimension_semantics=("arbitrary",), vmem_limit_bytes=VMEM_LIMIT_BYTES),
        name="trunk_layer_final" if final else "trunk_layer",
    )(x, *params)


def _expansion(n_compact, n_full, compact_of_full):
    e = np.zeros((n_compact, n_full), np.float32)
    e[compact_of_full, np.arange(n_full)] = 1.0
    return e


def _s5_constants():
    gpb = LANES // S5_GROUP
    q, h, p = np.meshgrid(np.arange(2), np.arange(gpb), np.arange(S5_STATE), indexing='ij')
    e_state = _expansion(2 * S5_STATE, 2 * S5_BLOCK_STATE, (q * S5_STATE + p).ravel())
    i, g, c = np.meshgrid(np.arange(S5_STEP), np.arange(gpb), np.arange(S5_GROUP), indexing='ij')
    e_chan = _expansion(S5_STEP * S5_GROUP, S5_STEP * LANES, (i * S5_GROUP + c).ravel())
    same_state = (g.ravel()[:, None] == h.ravel()[None, :]).astype(np.float32)
    same_chan = (g.ravel()[:, None] == g.ravel()[None, :]).astype(np.float32)
    return e_state, e_chan, same_state, same_chan


def _s5_params(lam_re, lam_im, log_dt, b_re, b_im, c_re, c_im):
    dt = jnp.exp(log_dt)[:, None]
    mag = jnp.exp(lam_re * dt)
    ab_re = mag * jnp.cos(lam_im * dt)
    ab_im = mag * jnp.sin(lam_im * dt)
    den = lam_re * lam_re + lam_im * lam_im
    nr = ab_re - 1.0
    ni = ab_im
    k_re = (nr * lam_re + ni * lam_im) / den
    k_im = (ni * lam_re - nr * lam_im) / den
    bb_re = k_re[..., None] * b_re - k_im[..., None] * b_im
    bb_im = k_re[..., None] * b_im + k_im[..., None] * b_re
    gpb = LANES // S5_GROUP

    pw = [(jnp.ones_like(ab_re), jnp.zeros_like(ab_im))]
    for _ in range(S5_STEP):
        p_re, p_im = pw[-1]
        pw.append((p_re * ab_re - p_im * ab_im, p_re * ab_im + p_im * ab_re))
    p_re = jnp.stack([p[0] for p in pw])
    p_im = jnp.stack([p[1] for p in pw])

    anb_re = p_re[..., None] * bb_re - p_im[..., None] * bb_im
    anb_im = p_re[..., None] * bb_im + p_im[..., None] * bb_re
    can_re = c_re * p_re[:, :, None, :] - c_im * p_im[:, :, None, :]
    can_im = c_re * p_im[:, :, None, :] + c_im * p_re[:, :, None, :]

    def by_block(m):
        return m.reshape(m.shape[:2] + (S5_BLOCKS, gpb) + m.shape[3:])

    n_in = jnp.arange(S5_STEP - 1, -1, -1)
    wk_c = by_block(jnp.stack([anb_re[n_in], anb_im[n_in]]))
    wk_c = wk_c.transpose(2, 1, 3, 5, 0, 4).reshape(S5_BLOCKS, S5_STEP * LANES, 2 * S5_STATE)
    n_out = jnp.arange(1, S5_STEP + 1)
    wo_c = by_block(jnp.stack([can_re[n_out], -can_im[n_out]]))
    wo_c = wo_c.transpose(2, 0, 5, 1, 3, 4).reshape(S5_BLOCKS, 2 * S5_STATE, S5_STEP * LANES)
    lags = slice(0, S5_STEP)
    t_n = jnp.sum(can_re[lags, :, None, :, :] * bb_re.transpose(0, 2, 1)[None, :, :, None, :]
                  - can_im[lags, :, None, :, :] * bb_im.transpose(0, 2, 1)[None, :, :, None, :],
                  axis=-1)
    lag = np.arange(S5_STEP)[None, :] - np.arange(S5_STEP)[:, None]
    tz_c = jnp.where((lag >= 0)[:, :, None, None, None], t_n[np.maximum(lag, 0)], 0.0)
    tz_c = by_block(tz_c).transpose(2, 0, 3, 4, 1, 5).reshape(S5_BLOCKS, S5_STEP * LANES, S5_STEP * S5_GROUP)

    def rep(a):
        return jnp.broadcast_to(a.reshape(S5_BLOCKS, 1, S5_BLOCK_STATE), (S5_BLOCKS, SUBLANES, S5_BLOCK_STATE))
    return wk_c.astype(BF16), rep(p_re[S5_STEP]), rep(p_im[S5_STEP]), wo_c.astype(BF16), tz_c.astype(BF16)


def kernel(x, norm_g, w_in, s5_lam_re, s5_lam_im, s5_log_dt, s5_b_re, s5_b_im, s5_c_re, s5_c_im, s5_d, s5_w_glu, s5_b_glu, pool_w, pool_scale, sgu_ln_g, sgu_ln_b, sgu_w, sgu_b, w_branch, w_out, final_norm_g):
    bsz, seq, d = x.shape
    assert (bsz, seq, d) == (BATCH, SEQ, D_MODEL)
    wk, a_re, a_im, wo, tz = jax.vmap(_s5_params)(s5_lam_re, s5_lam_im, s5_log_dt, s5_b_re, s5_b_im,
                                                  s5_c_re, s5_c_im)
    params = (
        norm_g.reshape(DEPTH, 1, d), w_in.astype(BF16), wk, a_re, a_im, wo, tz,
        s5_d.reshape(DEPTH, 1, D_BRANCH), s5_w_glu.astype(BF16), s5_b_glu.reshape(DEPTH, 1, D_BRANCH),
        pool_w.astype(BF16), pool_scale.reshape(DEPTH, 1, D_BRANCH),
        sgu_ln_g.reshape(DEPTH, 1, D_BRANCH), sgu_ln_b.reshape(DEPTH, 1, D_BRANCH),
        sgu_w, sgu_b.reshape(DEPTH, SGU_HEADS, SGU_BLOCK, 1),
        w_branch.astype(BF16), w_out.astype(BF16), final_norm_g.reshape(1, 1, d),
    )
    e_state, e_chan, same_state, same_chan = _s5_constants()
    params += tuple(jnp.asarray(c[None], BF16) for c in (
        e_state, e_state.T, e_chan, same_state, same_state.T, same_chan))
    for l in range(DEPTH):
        x = _layer_call(x, params, l, first=(l == 0), final=(l == DEPTH - 1))
    return x
```
